```python
import jax, jax.numpy as jnp
from jax import lax
import numpy as np

D_MODEL = 2048
BATCH = 4
SEQ = 8192
DEPTH = 1

CHUNK = 64
HG_HEADS = 8
HG_DK = 128
HG_DV = 128
HG_WIDTH = HG_HEADS * HG_DK
HG_VWIDTH = HG_HEADS * HG_DV
FOX_HEADS = 8
FOX_HD = 128
FOX_WIDTH = FOX_HEADS * FOX_HD
Q_BLOCK = 128
N_EXPERTS = 32
TOP_K = 4
D_EXPERT = D_MODEL
SWIGLU_LIMIT = 7.0
SWIGLU_ALPHA = 1.702
EXPERT_BLOCK = 512
EPS = 1e-6
SPLIT_WIDTHS = (HG_WIDTH, HG_WIDTH, HG_VWIDTH, HG_VWIDTH, FOX_WIDTH, FOX_WIDTH, FOX_WIDTH, FOX_HEADS, D_MODEL, D_MODEL)
IN_COLS = sum(SPLIT_WIDTHS)
SPLIT_POINTS = tuple(int(v) for v in np.cumsum(SPLIT_WIDTHS)[:-1])

kernel_name = "hgrn2_fox_gated_merge_moe_block"


def rms_norm(x, gain):
    xf = x.astype(jnp.float32)
    y = xf * lax.rsqrt(jnp.mean(xf * xf, axis=-1, keepdims=True) + EPS)
    return (y * gain.astype(jnp.float32)).astype(x.dtype)


def hgrn2_mixer(q, f_logit, i, g, lb, norm_gain):
    B, S = q.shape[0], q.shape[1]
    n_chunks = S // CHUNK
    qf = jax.nn.silu(q.astype(jnp.float32)) * (HG_DK ** -0.5)
    lbh = lb.reshape(HG_HEADS, HG_DK)
    f = lbh + (1.0 - lbh) * jax.nn.sigmoid(f_logit.astype(jnp.float32))
    k = 1.0 - f
    logf = jnp.log(f)

    def to_chunks(t):
        return t.reshape(B, n_chunks, CHUNK, HG_HEADS, -1).transpose(1, 0, 3, 2, 4)

    qc, kc, vc = to_chunks(qf), to_chunks(k), to_chunks(i.astype(jnp.float32))
    bc = jnp.cumsum(to_chunks(logf), axis=3)
    causal = jnp.tril(jnp.ones((CHUNK, CHUNK), dtype=bool))[:, :, None]

    def step(state, inp):
        qt, kt, vt, bt = inp
        diff = bt[:, :, :, None, :] - bt[:, :, None, :, :]
        decay = jnp.exp(jnp.where(causal, diff, -jnp.inf))
        scores = jnp.einsum('bhtc,bhsc,bhtsc->bhts', qt, kt, decay)
        o = jnp.einsum('bhts,bhsv->bhtv', scores, vt) + jnp.einsum('bhtc,bhcv->bhtv', qt * jnp.exp(bt), state)
        b_last = bt[:, :, -1:, :]
        new_state = jnp.exp(b_last[:, :, 0, :])[..., None] * state + jnp.einsum(
            'bhsc,bhsv->bhcv', kt * jnp.exp(b_last - bt), vt)
        return new_state, o

    state0 = jnp.zeros((B, HG_HEADS, HG_DK, HG_DV), jnp.float32)
    _, oc = lax.scan(step, state0, (qc, kc, vc, bc))
    o = oc.transpose(1, 0, 3, 2, 4).reshape(B, S, HG_HEADS, HG_DV)
    o = rms_norm(o, norm_gain) * jax.nn.silu(g.astype(jnp.float32))
    return o.reshape(B, S, HG_VWIDTH)


def fox_mixer(q, k, v, f_logit, q_gain, k_gain):
    B, S = q.shape[0], q.shape[1]
    n_q = S // Q_BLOCK
    qn = rms_norm(q, q_gain).astype(jnp.float32) * (FOX_HD ** -0.5)
    kh = rms_norm(k, k_gain).astype(jnp.float32).transpose(0, 2, 1, 3)
    vh = v.astype(jnp.float32).transpose(0, 2, 1, 3)
    c = jnp.cumsum(jax.nn.log_sigmoid(f_logit.astype(jnp.float32)), axis=1).transpose(0, 2, 1)
    qb = qn.reshape(B, n_q, Q_BLOCK, FOX_HEADS, FOX_HD).transpose(1, 0, 3, 2, 4)
    cb = c.reshape(B, FOX_HEADS, n_q, Q_BLOCK).transpose(2, 0, 1, 3)
    key_pos = jnp.arange(S)

    def block(args):
        q_blk, c_blk, idx = args
        q_pos = idx * Q_BLOCK + jnp.arange(Q_BLOCK)
        s = jnp.einsum('bhqd,bhkd->bhqk', q_blk, kh) + c_blk[..., None] - c[:, :, None, :]
        s = jnp.where(key_pos[None, :] <= q_pos[:, None], s, -jnp.inf)
        p = jax.nn.softmax(s, axis=-1)
        return jnp.einsum('bhqk,bhkd->bhqd', p, vh)

    o = lax.map(block, (qb, cb, jnp.arange(n_q)))
    return o.transpose(1, 0, 3, 2, 4).reshape(B, S, FOX_WIDTH)


def moe_ffn(h, router_w, router_b, w_gu, b_gu, w_down, b_down):
    B, S, D = h.shape
    T = B * S
    TK = T * TOP_K
    ht = h.reshape(T, D)
    logits = (ht @ router_w + router_b).astype(jnp.float32)
    top_v, top_i = lax.top_k(logits, TOP_K)
    gates = jax.nn.softmax(top_v, axis=-1)
    flat_e = top_i.reshape(TK)
    order = jnp.argsort(flat_e)
    sorted_e = flat_e[order]
    sorted_tok = (order // TOP_K).astype(jnp.int32)
    counts = jnp.bincount(flat_e, length=N_EXPERTS)
    padded = (counts + EXPERT_BLOCK - 1) // EXPERT_BLOCK * EXPERT_BLOCK
    pad_end = jnp.cumsum(padded)
    pad_start = pad_end - padded
    start = jnp.cumsum(counts) - counts
    dest = pad_start[sorted_e] + jnp.arange(TK) - start[sorted_e]
    n_blocks = -(-TK // EXPERT_BLOCK) + N_EXPERTS
    n_rows = n_blocks * EXPERT_BLOCK
    row_tok = jnp.zeros((n_rows,), jnp.int32).at[dest].set(sorted_tok)
    row_w = jnp.zeros((n_rows,), jnp.float32).at[dest].set(gates.reshape(TK)[order])
    block_e = jnp.minimum(jnp.searchsorted(pad_end, jnp.arange(n_blocks) * EXPERT_BLOCK, side='right'),
                          N_EXPERTS - 1)

    def expert_block(args):
        tok, rw, e = args
        xb = ht[tok]
        gu = xb @ w_gu[e] + b_gu[e]
        gate, up = jnp.split(gu, 2, axis=-1)
        gate = jnp.minimum(gate, SWIGLU_LIMIT)
        up = jnp.clip(up, -SWIGLU_LIMIT, SWIGLU_LIMIT)
        act = (up + 1.0) * gate * jax.nn.sigmoid(SWIGLU_ALPHA * gate)
        out = act @ w_down[e] + b_down[e]
        return out * rw[:, None].astype(out.dtype)

    rows = lax.map(expert_block, (row_tok.reshape(n_blocks, EXPERT_BLOCK),
                                  row_w.reshape(n_blocks, EXPERT_BLOCK), block_e))
    y = jax.ops.segment_sum(rows.reshape(n_rows, D), row_tok, num_segments=T)
    return y.reshape(B, S, D)


def setup_inputs(seed: int = 0) -> dict:
    key = jax.random.key(seed)
    ks = jax.random.split(key, 20)
    f32 = jnp.float32
    n = lambda k, shape, s: jax.random.normal(k, shape, f32) * s
    return {
        "x": n(ks[0], (BATCH, SEQ, D_MODEL), 1.0),
        "lb_logits": n(ks[1], (DEPTH + 1, HG_WIDTH), 0.1),
        "norm1": 1.0 + n(ks[2], (DEPTH, D_MODEL), 0.02),
        "w_in": n(ks[3], (DEPTH, D_MODEL, IN_COLS), D_MODEL ** -0.5),
        "hg_norm": 1.0 + n(ks[4], (DEPTH, HG_DV), 0.02),
        "fox_q_norm": 1.0 + n(ks[5], (DEPTH, FOX_HD), 0.02),
        "fox_k_norm": 1.0 + n(ks[6], (DEPTH, FOX_HD), 0.02),
        "fox_f_bias": 2.0 + n(ks[7], (DEPTH, FOX_HEADS), 1.0),
        "w_up_hg": n(ks[8], (DEPTH, HG_VWIDTH, D_MODEL), HG_VWIDTH ** -0.5),
        "w_up_fox": n(ks[9], (DEPTH, FOX_WIDTH, D_MODEL), FOX_WIDTH ** -0.5),
        "w_out": n(ks[10], (DEPTH, D_MODEL, D_MODEL), D_MODEL ** -0.5),
        "norm2": 1.0 + n(ks[11], (DEPTH, D_MODEL), 0.02),
        "router_w": n(ks[12], (DEPTH, D_MODEL, N_EXPERTS), D_MODEL ** -0.5),
        "router_b": n(ks[13], (DEPTH, N_EXPERTS), 0.01),
        "w_gu": n(ks[14], (DEPTH, N_EXPERTS, D_MODEL, 2 * D_EXPERT), D_MODEL ** -0.5),
        "b_gu": n(ks[15], (DEPTH, N_EXPERTS, 2 * D_EXPERT), 0.01),
        "w_down": n(ks[16], (DEPTH, N_EXPERTS, D_EXPERT, D_MODEL), D_EXPERT ** -0.5),
        "b_down": n(ks[17], (DEPTH, N_EXPERTS, D_MODEL), 0.01),
    }


def reference(x, lb_logits, norm1, w_in, hg_norm, fox_q_norm, fox_k_norm, fox_f_bias, w_up_hg, w_up_fox,
              w_out, norm2, router_w, router_b, w_gu, b_gu, w_down, b_down):
    B, S, _ = x.shape
    lb_all = jnp.cumsum(jax.nn.softmax(lb_logits.astype(jnp.float32), axis=0), axis=0)
    for l in range(DEPTH):
        h = rms_norm(x, norm1[l])
        p = h @ w_in[l]
        hq, hf, hi, hg, fq, fk, fv, ff, ga, gb = jnp.split(p, SPLIT_POINTS, axis=-1)
        heads4 = lambda t, hd: t.reshape(B, S, -1, hd)
        a = hgrn2_mixer(heads4(hq, HG_DK), heads4(hf, HG_DK), heads4(hi, HG_DV), heads4(hg, HG_DV),
                        lb_all[l], hg_norm[l])
        b = fox_mixer(heads4(fq, FOX_HD), heads4(fk, FOX_HD), heads4(fv, FOX_HD), ff + fox_f_bias[l],
                      fox_q_norm[l], fox_k_norm[l])
        merged = (jax.nn.sigmoid(ga.astype(jnp.float32)) * (a.astype(x.dtype) @ w_up_hg[l])
                  + jax.nn.sigmoid(gb.astype(jnp.float32)) * (b.astype(x.dtype) @ w_up_fox[l]))
        x = x + (merged.astype(x.dtype) @ w_out[l])
        h2 = rms_norm(x, norm2[l])
        x = x + moe_ffn(h2, router_w[l], router_b[l], w_gu[l], b_gu[l], w_down[l], b_down[l]).astype(x.dtype)
    return x
```

```python
import functools

import jax
import jax.numpy as jnp
from jax import lax
from jax.experimental import pallas as pl
from jax.experimental.pallas import tpu as pltpu

_F32 = jnp.float32
_BF16 = jnp.bfloat16
_I32 = jnp.int32

V7X_LANES = 128
V7X_SUBLANES = 8
V7X_VMEM_BYTES = 64 * 1024 * 1024
VMEM_LIMIT_BYTES = V7X_VMEM_BYTES * 7 // 8

EPS = 1e-6
SWIGLU_LIMIT = 7.0
SWIGLU_ALPHA = 1.702
TOP_K = 4
TOP_K_SHIFT = TOP_K.bit_length() - 1
assert 1 << TOP_K_SHIFT == TOP_K
NEG_BIG = -1e30

HGRN_ROWS = 256
HGRN_DIAG = V7X_SUBLANES
EXPERT_ROWS = 512


def _params(*semantics):
    return pltpu.CompilerParams(dimension_semantics=semantics, vmem_limit_bytes=VMEM_LIMIT_BYTES)


def _pick(n, candidates):
    for c in candidates:
        if n % c == 0:
            return c
    raise ValueError(f"no tile in {candidates} divides {n}")


def _sigmoid(x):
    return 1.0 / (1.0 + jnp.exp(-x))


def _split3(x):
    hi = x.astype(_BF16)
    r = x - hi.astype(_F32)
    mid = r.astype(_BF16)
    lo = (r - mid.astype(_F32)).astype(_BF16)
    return hi, mid, lo


def _dot_exact_lhs(lhs_bf16, x):
    out = None
    for part in _split3(x):
        t = jnp.dot(lhs_bf16, part, preferred_element_type=_F32)
        out = t if out is None else out + t
    return out


def _dot_f32(a, b):
    a_hi, a_lo, _ = _split3(a)
    b_hi, b_lo, _ = _split3(b)
    return (jnp.dot(a_hi, b_hi, preferred_element_type=_F32)
            + jnp.dot(a_hi, b_lo, preferred_element_type=_F32)
            + jnp.dot(a_lo, b_hi, preferred_element_type=_F32))


def _rms(x, gain):
    return x * lax.rsqrt(jnp.mean(x * x, axis=-1, keepdims=True) + EPS) * gain


def _inproj_body(x_ref, g_ref, w_ref, wff_ref, p_ref, ff_ref, h_scr):
    @pl.when(pl.program_id(1) == 0)
    def _():
        h = _rms(x_ref[...], g_ref[...])
        h_scr[...] = h.astype(_BF16)
        ff_ref[...] = _dot_f32(h, wff_ref[...])

    acc = jnp.dot(h_scr[...], w_ref[...], preferred_element_type=_F32)
    for s in range(p_ref.shape[0]):
        p_ref[s] = acc[:, s * V7X_LANES:(s + 1) * V7X_LANES].astype(_BF16)


def _in_proj(x2d, gain, w_main, w_ff):
    t, d = x2d.shape
    n = w_main.shape[1]
    tm = _pick(t, (1024, 512, 256, 128))
    tn = _pick(n, (1024, 512, 256, 128))
    slabs = tn // V7X_LANES
    return pl.pallas_call(
        _inproj_body,
        grid=(t // tm, n // tn),
        in_specs=[
            pl.BlockSpec((tm, d), lambda i, j: (i, 0)),
            pl.BlockSpec((1, d), lambda i, j: (0, 0)),
            pl.BlockSpec((d, tn), lambda i, j: (0, j)),
            pl.BlockSpec((d, V7X_LANES), lambda i, j: (0, 0)),
        ],
        out_specs=[
            pl.BlockSpec((slabs, tm, V7X_LANES), lambda i, j: (j, i, 0)),
            pl.BlockSpec((tm, V7X_LANES), lambda i, j: (i, 0)),
        ],
        out_shape=[
            jax.ShapeDtypeStruct((n // V7X_LANES, t, V7X_LANES), _BF16),
            jax.ShapeDtypeStruct((t, V7X_LANES), _F32),
        ],
        scratch_shapes=[pltpu.VMEM((tm, d), _BF16)],
        compiler_params=_params("arbitrary", "arbitrary"),
        name="in_proj",
    )(x2d, gain, w_main, w_ff)


def _hgrn_body(lbl_ref, q_ref, f_ref, i_ref, g_ref, gain_ref, o_ref,
               st_ref, b_scr, q_scr, k_scr, d_scr, *, layer, rows, dk):
    @pl.when(pl.program_id(2) == 0)
    def _():
        st_ref[...] = jnp.zeros_like(st_ref)

    lbl = lbl_ref[...]
    e = jnp.exp(lbl - jnp.max(lbl, axis=0, keepdims=True))
    lb = jnp.sum(e[:layer + 1], axis=0, keepdims=True) / jnp.sum(e, axis=0, keepdims=True)

    q = q_ref[0].astype(_F32)
    qf = q * _sigmoid(q) * (dk ** -0.5)
    fg = lb + (1.0 - lb) * _sigmoid(f_ref[0].astype(_F32))
    k = 1.0 - fg
    v = i_ref[0]

    row = lax.broadcasted_iota(_I32, (rows, rows), 0)
    col = lax.broadcasted_iota(_I32, (rows, rows), 1)
    b = _dot_exact_lhs((row >= col).astype(_BF16), jnp.log(fg))

    b_scr[...] = b
    q_scr[...] = qf
    k_scr[...] = k

    def diag_block(gi, carry):
        r0 = pl.multiple_of(gi * HGRN_DIAG, HGRN_DIAG)
        bb = b_scr[pl.ds(r0, HGRN_DIAG), :]
        qq = q_scr[pl.ds(r0, HGRN_DIAG), :]
        kk = k_scr[pl.ds(r0, HGRN_DIAG), :]
        sub = lax.broadcasted_iota(_I32, bb.shape, 0)
        colb = lax.broadcasted_iota(_I32, (HGRN_DIAG, rows), 1)
        acc = jnp.zeros((HGRN_DIAG, rows), _F32)
        for sl in range(HGRN_DIAG):
            decay = jnp.exp(jnp.where(sub >= sl, bb - bb[sl:sl + 1, :], NEG_BIG))
            sc = jnp.sum(qq * decay * kk[sl:sl + 1, :], axis=-1, keepdims=True)
            acc = jnp.where(colb == r0 + sl, sc, acc)
        d_scr[pl.ds(r0, HGRN_DIAG), :] = acc
        return carry

    lax.fori_loop(0, rows // HGRN_DIAG, diag_block, 0)

    scores = d_scr[...]
    rowv = lax.broadcasted_iota(_I32, (rows, V7X_LANES), 0)
    x = rows // 2
    while x >= HGRN_DIAG:
        mids = [jnp.broadcast_to(b[m + x - 1:m + x, :], (2 * x, b.shape[1]))
                for m in range(0, rows, 2 * x)]
        bm = mids[0] if len(mids) == 1 else jnp.concatenate(mids, axis=0)
        upper = (rowv & x) != 0
        w = jnp.exp(-jnp.abs(b - bm))
        qx = jnp.where(upper, qf * w, 0.0).astype(_BF16)
        kx = jnp.where(upper, 0.0, k * w).astype(_BF16)
        sx = lax.dot_general(qx, kx, (((1,), (1,)), ((), ())), preferred_element_type=_F32)
        if 2 * x < rows:
            shift = (2 * x).bit_length() - 1
            sx = jnp.where((row >> shift) == (col >> shift), sx, 0.0)
        scores = scores + sx
        x //= 2

    o = jnp.dot(scores.astype(_BF16), v, preferred_element_type=_F32)

    st = st_ref[...]
    q_in = (qf * jnp.exp(b)).astype(_BF16)
    o = o + lax.dot_general(q_in, st.astype(_BF16), (((1,), (1,)), ((), ())),
                            preferred_element_type=_F32)
    b_last = b[rows - 1:rows, :]
    k_out = (k * jnp.exp(b_last - b)).astype(_BF16)
    upd = lax.dot_general(v, k_out, (((0,), (0,)), ((), ())), preferred_element_type=_F32)
    st_ref[...] = st * jnp.exp(b_last) + upd

    g = g_ref[0].astype(_F32)
    o_ref[...] = (_rms(o, gain_ref[...]) * (g * _sigmoid(g))).astype(o_ref.dtype)


def _hgrn2(p, lb_logits, gain, *, layer, batch, seq, heads, slab_q, slab_f, slab_i, slab_g):
    t = batch * seq
    hd = p.shape[2]
    rows = min(HGRN_ROWS, seq)
    nblk = seq // rows

    def slab(first):
        return pl.BlockSpec((1, rows, hd), lambda b, h, n: (first + h, b * nblk + n, 0))

    return pl.pallas_call(
        functools.partial(_hgrn_body, layer=layer, rows=rows, dk=hd),
        grid=(batch, heads, nblk),
        in_specs=[
            pl.BlockSpec((lb_logits.shape[0], hd), lambda b, h, n: (0, h)),
            slab(slab_q), slab(slab_f), slab(slab_i), slab(slab_g),
            pl.BlockSpec((1, hd), lambda b, h, n: (0, 0)),
        ],
        out_specs=pl.BlockSpec((rows, hd), lambda b, h, n: (b * nblk + n, h)),
        out_shape=jax.ShapeDtypeStruct((t, heads * hd), _BF16),
        scratch_shapes=[
            pltpu.VMEM((hd, hd), _F32),
            pltpu.VMEM((rows, hd), _F32),
            pltpu.VMEM((rows, hd), _F32),
            pltpu.VMEM((rows, hd), _F32),
            pltpu.VMEM((rows, rows), _F32),
        ],
        compiler_params=_params("arbitrary", "arbitrary", "arbitrary"),
        name="hgrn2",
    )(lb_logits, p, p, p, p, gain)


def _fox_decay_body(ff_ref, bias_ref, ccol_ref, crow_ref, carry_ref, *, heads_pad):
    @pl.when(pl.program_id(1) == 0)
    def _():
        carry_ref[...] = jnp.zeros_like(carry_ref)

    x = ff_ref[...] + bias_ref[...]
    logsig = jnp.minimum(x, 0.0) - jnp.log(1.0 + jnp.exp(-jnp.abs(x)))
    rows = x.shape[0]
    row = lax.broadcasted_iota(_I32, (rows, rows), 0)
    col = lax.broadcasted_iota(_I32, (rows, rows), 1)
    c = _dot_exact_lhs((row >= col).astype(_BF16), logsig) + carry_ref[...]
    ccol_ref[...] = c
    crow_ref[...] = jnp.transpose(c)[:heads_pad, :]
    carry_ref[...] = c[rows - 1:rows, :]


def _fox_decay(ff, bias_pad, *, batch, seq, heads_pad):
    t = batch * seq
    rows = _pick(seq, (512, 256, 128))
    nblk = seq // rows
    return pl.pallas_call(
        functools.partial(_fox_decay_body, heads_pad=heads_pad),
        grid=(batch, nblk),
        in_specs=[
            pl.BlockSpec((rows, V7X_LANES), lambda b, n: (b * nblk + n, 0)),
            pl.BlockSpec((1, V7X_LANES), lambda b, n: (0, 0)),
        ],
        out_specs=[
            pl.BlockSpec((rows, V7X_LANES), lambda b, n: (b * nblk + n, 0)),
            pl.BlockSpec((None, heads_pad, rows), lambda b, n: (b, 0, n)),
        ],
        out_shape=[
            jax.ShapeDtypeStruct((t, V7X_LANES), _F32),
            jax.ShapeDtypeStruct((batch, heads_pad, seq), _F32),
        ],
        scratch_shapes=[pltpu.VMEM((1, V7X_LANES), _F32)],
        compiler_params=_params("arbitrary", "arbitrary"),
        name="fox_decay",
    )(ff, bias_pad)


def _fox_body(q_ref, k_ref, v_ref, ccol_ref, crow_ref, qg_ref, kg_ref, o_ref,
              kn_scr, m_scr, l_scr, acc_scr, *, tq, seq):
    h = pl.program_id(1)
    qi = pl.program_id(2)
    hd = q_ref.shape[2]

    @pl.when(qi == 0)
    def _():
        def norm_block(c, carry):
            r0 = pl.multiple_of(c * tq, tq)
            kk = k_ref[0, pl.ds(r0, tq), :].astype(_F32)
            kn_scr[pl.ds(r0, tq), :] = _rms(kk, kg_ref[...]).astype(_BF16)
            return carry
        lax.fori_loop(0, seq // tq, norm_block, 0)

    qn = (_rms(q_ref[0].astype(_F32), qg_ref[...]) * (hd ** -0.5)).astype(_BF16)
    lane = lax.broadcasted_iota(_I32, ccol_ref.shape, 1)
    cq = jnp.sum(jnp.where(lane == h, ccol_ref[...], 0.0), axis=-1, keepdims=True)

    m_scr[...] = jnp.full_like(m_scr, NEG_BIG)
    l_scr[...] = jnp.zeros_like(l_scr)
    acc_scr[...] = jnp.zeros_like(acc_scr)

    def step(j, masked):
        r0 = pl.multiple_of(j * tq, tq)
        s = lax.dot_general(qn, kn_scr[pl.ds(r0, tq), :], (((1,), (1,)), ((), ())),
                            preferred_element_type=_F32)
        s = s - crow_ref[0, pl.ds(j, 1), :]
        if masked:
            row = lax.broadcasted_iota(_I32, s.shape, 0)
            col = lax.broadcasted_iota(_I32, s.shape, 1)
            s = jnp.where(col <= row, s, NEG_BIG)
        m_old = m_scr[...]
        m_new = jnp.maximum(m_old, jnp.max(s, axis=-1, keepdims=True) + cq)
        alpha = jnp.exp(m_old - m_new)
        p = jnp.exp(s + (cq - m_new))
        l_scr[...] = alpha * l_scr[...] + jnp.sum(p, axis=-1, keepdims=True)
        acc_scr[...] = alpha * acc_scr[...] + jnp.dot(
            p.astype(_BF16), v_ref[0, pl.ds(r0, tq), :], preferred_element_type=_F32)
        m_scr[...] = m_new

    def full_step(j, carry):
        step(j, False)
        return carry

    lax.fori_loop(0, qi, full_step, 0)
    step(qi, True)
    o_ref[...] = (acc_scr[...] / l_scr[...]).astype(o_ref.dtype)


def _fox_attn(p, c_col, c_row, q_gain, k_gain, *, batch, seq, heads, slab_q, slab_k, slab_v):
    t = batch * seq
    hd = p.shape[2]
    tq = c_row.shape[2]
    nq = seq // tq
    return pl.pallas_call(
        functools.partial(_fox_body, tq=tq, seq=seq),
        grid=(batch, heads, nq),
        in_specs=[
            pl.BlockSpec((1, tq, hd), lambda b, h, i: (slab_q + h, b * nq + i, 0)),
            pl.BlockSpec((1, seq, hd), lambda b, h, i: (slab_k + h, b, 0)),
            pl.BlockSpec((1, seq, hd), lambda b, h, i: (slab_v + h, b, 0)),
            pl.BlockSpec((tq, V7X_LANES), lambda b, h, i: (b * nq + i, 0)),
            pl.BlockSpec((1, nq, tq), lambda b, h, i: (b * heads + h, 0, 0)),
            pl.BlockSpec((1, hd), lambda b, h, i: (0, 0)),
            pl.BlockSpec((1, hd), lambda b, h, i: (0, 0)),
        ],
        out_specs=pl.BlockSpec((tq, hd), lambda b, h, i: (b * nq + i, h)),
        out_shape=jax.ShapeDtypeStruct((t, heads * hd), _BF16),
        scratch_shapes=[
            pltpu.VMEM((seq, hd), _BF16),
            pltpu.VMEM((tq, 1), _F32),
            pltpu.VMEM((tq, 1), _F32),
            pltpu.VMEM((tq, hd), _F32),
        ],
        compiler_params=_params("arbitrary", "arbitrary", "arbitrary"),
        name="fox_attn",
    )(p, p, p, c_col, c_row, q_gain, k_gain)


def _merge_body(a_ref, b_ref, ga_ref, gb_ref, x_ref, wa_ref, wb_ref, wo_ref, n2_ref, rw_ref,
                rb_ref, x2_ref, h2_ref, route_ref, gate_ref, cnt_ref, carry_ref):
    @pl.when(pl.program_id(0) == 0)
    def _():
        carry_ref[...] = jnp.zeros_like(carry_ref)

    def gates(ref):
        parts = [ref[s] for s in range(ref.shape[0])]
        g = parts[0] if len(parts) == 1 else jnp.concatenate(parts, axis=-1)
        return _sigmoid(g.astype(_F32))

    ua = jnp.dot(a_ref[...], wa_ref[...], preferred_element_type=_F32)
    ub = jnp.dot(b_ref[...], wb_ref[...], preferred_element_type=_F32)
    merged = gates(ga_ref) * ua + gates(gb_ref) * ub
    x2 = x_ref[...] + jnp.dot(merged.astype(_BF16), wo_ref[...], preferred_element_type=_F32)
    x2_ref[...] = x2
    h2 = _rms(x2, n2_ref[...])
    h2_ref[...] = h2

    logits = _dot_f32(h2, rw_ref[...]) + rb_ref[...]
    tm = logits.shape[0]
    lane = lax.broadcasted_iota(_I32, logits.shape, 1)
    lane_f = lane.astype(_F32)
    work = logits
    top_v, top_i = [], []
    for _ in range(TOP_K):
        mx = jnp.max(work, axis=-1, keepdims=True)
        ix = jnp.min(jnp.where(work == mx, lane_f, float(V7X_LANES)), axis=-1,
                     keepdims=True).astype(_I32)
        top_v.append(mx)
        top_i.append(ix)
        work = jnp.where(lane == ix, NEG_BIG, work)

    ex = [jnp.exp(tv - top_v[0]) for tv in top_v]
    denom = ex[0]
    for t_ in ex[1:]:
        denom = denom + t_

    onehot = jnp.zeros(logits.shape, _F32)
    for ix in top_i:
        onehot = onehot + (lane == ix).astype(_F32)
    row = lax.broadcasted_iota(_I32, (tm, tm), 0)
    col = lax.broadcasted_iota(_I32, (tm, tm), 1)
    before = jnp.dot((row > col).astype(_BF16), onehot.astype(_BF16),
                     preferred_element_type=_F32) + carry_ref[...]

    route = jnp.zeros(logits.shape, _I32)
    gate = jnp.zeros(logits.shape, _F32)
    for r in range(TOP_K):
        rank = jnp.sum(jnp.where(lane == top_i[r], before, 0.0), axis=-1, keepdims=True)
        route = jnp.where(lane == r, top_i[r], route)
        route = jnp.where(lane == TOP_K + r, rank.astype(_I32), route)
        gate = jnp.where(lane == r, ex[r] / denom, gate)
    route_ref[...] = route
    gate_ref[...] = gate

    carry = carry_ref[...] + jnp.sum(onehot, axis=0, keepdims=True)
    carry_ref[...] = carry
    cnt_ref[...] = carry.astype(_I32)


def _merge(a, b, p, x2d, wa, wb, wo, n2, rw_pad, rb_pad, *, slab_ga, slab_gb):
    t, d = x2d.shape
    tm = _pick(t, (256, 128))
    gs = d // V7X_LANES
    const = lambda i: (0, 0)
    once = pl.Buffered(1)
    return pl.pallas_call(
        _merge_body,
        grid=(t // tm,),
        in_specs=[
            pl.BlockSpec((tm, a.shape[1]), lambda i: (i, 0)),
            pl.BlockSpec((tm, b.shape[1]), lambda i: (i, 0)),
            pl.BlockSpec((gs, tm, V7X_LANES), lambda i: (slab_ga // gs, i, 0)),
            pl.BlockSpec((gs, tm, V7X_LANES), lambda i: (slab_gb // gs, i, 0)),
            pl.BlockSpec((tm, d), lambda i: (i, 0)),
            pl.BlockSpec(wa.shape, const, pipeline_mode=once),
            pl.BlockSpec(wb.shape, const, pipeline_mode=once),
            pl.BlockSpec(wo.shape, const, pipeline_mode=once),
            pl.BlockSpec((1, d), const),
            pl.BlockSpec(rw_pad.shape, const, pipeline_mode=once),
            pl.BlockSpec((1, V7X_LANES), const),
        ],
        out_specs=[
            pl.BlockSpec((tm, d), lambda i: (i, 0)),
            pl.BlockSpec((tm, d), lambda i: (i, 0)),
            pl.BlockSpec((tm, V7X_LANES), lambda i: (i, 0)),
            pl.BlockSpec((tm, V7X_LANES), lambda i: (i, 0)),
            pl.BlockSpec((1, V7X_LANES), const),
        ],
        out_shape=[
            jax.ShapeDtypeStruct((t, d), _F32),
            jax.ShapeDtypeStruct((t, d), _F32),
            jax.ShapeDtypeStruct((t, V7X_LANES), _I32),
            jax.ShapeDtypeStruct((t, V7X_LANES), _F32),
            jax.ShapeDtypeStruct((1, V7X_LANES), _I32),
        ],
        scratch_shapes=[pltpu.VMEM((1, V7X_LANES), _F32)],
        compiler_params=_params("arbitrary"),
        name="merge_router",
    )(a, b, p, p, x2d, wa, wb, wo, n2, rw_pad, rb_pad)


def _row_copy(src, src_row, dst, dst_row, sem, rows=1):
    return pltpu.make_async_copy(src.at[pl.ds(src_row, rows)], dst.at[pl.ds(dst_row, rows)], sem)


def _dispatch_body(pstart_ref, cnt_ref, eidx_ref, rank_ref, h2_ref, zeros_ref, xs_ref,
                   sem, zsem, *, tokens, n_exp, n_rows):
    step = pl.program_id(0)
    n_assign = tokens * TOP_K

    def token_copy(a):
        dst = pstart_ref[eidx_ref[a]] + rank_ref[a]
        return _row_copy(h2_ref, step * tokens + (a >> TOP_K_SHIFT), xs_ref, dst, sem)

    def start(a, carry):
        token_copy(a).start()
        return carry

    def wait(a, carry):
        token_copy(a).wait()
        return carry

    def fill(do):
        def expert_pad(e, carry):
            first = pstart_ref[e] + cnt_ref[e]

            def pad_row(c, carry):
                do(_row_copy(zeros_ref, 0, xs_ref, first + c, zsem))
                return carry

            lax.fori_loop(0, pstart_ref[e + 1] - first, pad_row, 0)
            return carry

        lax.fori_loop(0, n_exp, expert_pad, 0)

        tail = pstart_ref[n_exp]

        def tail_block(c, carry):
            dst = pl.multiple_of(tail + c * EXPERT_ROWS, EXPERT_ROWS)
            do(_row_copy(zeros_ref, 0, xs_ref, dst, zsem, EXPERT_ROWS))
            return carry

        lax.fori_loop(0, (n_rows - tail) // EXPERT_ROWS, tail_block, 0)

    @pl.when(step == 0)
    def _():
        fill(lambda c: c.start())

    lax.fori_loop(0, n_assign, start, 0)
    lax.fori_loop(0, n_assign, wait, 0)

    @pl.when(step == 0)
    def _():
        fill(lambda c: c.wait())


def _dispatch(h2, eidx, rank, pstart, counts, zeros, *, n_rows):
    t, d = h2.shape
    tokens = _pick(t, (256, 128))
    n_exp = counts.shape[0]
    smem = lambda: pl.BlockSpec((tokens * TOP_K,), lambda i, *_: (i,), memory_space=pltpu.SMEM)
    return pl.pallas_call(
        functools.partial(_dispatch_body, tokens=tokens, n_exp=n_exp, n_rows=n_rows),
        grid_spec=pltpu.PrefetchScalarGridSpec(
            num_scalar_prefetch=2,
            grid=(t // tokens,),
            in_specs=[smem(), smem(),
                      pl.BlockSpec(memory_space=pl.ANY), pl.BlockSpec(memory_space=pl.ANY)],
            out_specs=pl.BlockSpec(memory_space=pl.ANY),
            scratch_shapes=[pltpu.SemaphoreType.DMA, pltpu.SemaphoreType.DMA],
        ),
        out_shape=jax.ShapeDtypeStruct((n_rows, d), h2.dtype),
        compiler_params=_params("arbitrary"),
        name="dispatch",
    )(pstart, counts, eidx, rank, h2, zeros)


def _expert_gu_body(bexp_ref, nvalid_ref, x_ref, wg_ref, wu_ref, bg_ref, bu_ref, o_ref):
    @pl.when(pl.program_id(1) < nvalid_ref[0])
    def _():
        x = x_ref[...].astype(_BF16)
        gate = jnp.dot(x, wg_ref[0].astype(_BF16), preferred_element_type=_F32) + bg_ref[0]
        up = jnp.dot(x, wu_ref[0].astype(_BF16), preferred_element_type=_F32) + bu_ref[0]
        gate = jnp.minimum(gate, SWIGLU_LIMIT)
        up = jnp.clip(up, -SWIGLU_LIMIT, SWIGLU_LIMIT)
        o_ref[...] = ((up + 1.0) * gate * _sigmoid(SWIGLU_ALPHA * gate)).astype(o_ref.dtype)

    @pl.when(pl.program_id(1) >= nvalid_ref[0])
    def _():
        o_ref[...] = jnp.zeros_like(o_ref)


def _expert_gu(xs, w_gu, b_gu, bexp, nvalid):
    n_rows, d = xs.shape
    n_exp, _, two_de = w_gu.shape
    de = two_de // 2
    tm = EXPERT_ROWS
    tn = _pick(de, (1024, 512, 256, 128))
    nj = de // tn
    nb = n_rows // tm

    def blk(i, nvalid_ref):
        return jnp.minimum(i, nvalid_ref[0] - 1)

    return pl.pallas_call(
        _expert_gu_body,
        grid_spec=pltpu.PrefetchScalarGridSpec(
            num_scalar_prefetch=2,
            grid=(nj, nb),
            in_specs=[
                pl.BlockSpec((tm, d), lambda j, i, be, nv: (blk(i, nv), 0)),
                pl.BlockSpec((1, d, tn), lambda j, i, be, nv: (be[blk(i, nv)], 0, j)),
                pl.BlockSpec((1, d, tn), lambda j, i, be, nv: (be[blk(i, nv)], 0, nj + j)),
                pl.BlockSpec((1, 1, tn), lambda j, i, be, nv: (be[blk(i, nv)], 0, j)),
                pl.BlockSpec((1, 1, tn), lambda j, i, be, nv: (be[blk(i, nv)], 0, nj + j)),
            ],
            out_specs=pl.BlockSpec((tm, tn), lambda j, i, be, nv: (i, j)),
        ),
        out_shape=jax.ShapeDtypeStruct((n_rows, de), _BF16),
        compiler_params=_params("arbitrary", "arbitrary"),
        name="expert_gu",
    )(bexp, nvalid, xs, w_gu, w_gu, b_gu.reshape(n_exp, 1, two_de), b_gu.reshape(n_exp, 1, two_de))


def _expert_down_body(bexp_ref, nvalid_ref, a_ref, w_ref, b_ref, o_ref):
    @pl.when(pl.program_id(1) < nvalid_ref[0])
    def _():
        o_ref[...] = jnp.dot(a_ref[...], w_ref[0].astype(_BF16),
                             preferred_element_type=_F32) + b_ref[0]

    @pl.when(pl.program_id(1) >= nvalid_ref[0])
    def _():
        o_ref[...] = jnp.zeros_like(o_ref)


def _expert_down(act, w_down, b_down, bexp, nvalid):
    n_rows, de = act.shape
    n_exp, _, d = w_down.shape
    tm = EXPERT_ROWS
    tn = _pick(d, (1024, 512, 256, 128))
    nj = d // tn
    nb = n_rows // tm

    def blk(i, nvalid_ref):
        return jnp.minimum(i, nvalid_ref[0] - 1)

    return pl.pallas_call(
        _expert_down_body,
        grid_spec=pltpu.PrefetchScalarGridSpec(
            num_scalar_prefetch=2,
            grid=(nj, nb),
            in_specs=[
                pl.BlockSpec((tm, de), lambda j, i, be, nv: (blk(i, nv), 0)),
                pl.BlockSpec((1, de, tn), lambda j, i, be, nv: (be[blk(i, nv)], 0, j)),
                pl.BlockSpec((1, 1, tn), lambda j, i, be, nv: (be[blk(i, nv)], 0, j)),
            ],
            out_specs=pl.BlockSpec((tm, tn), lambda j, i, be, nv: (i, j)),
        ),
        out_shape=jax.ShapeDtypeStruct((n_rows, d), _F32),
        compiler_params=_params("arbitrary", "arbitrary"),
        name="expert_down",
    )(bexp, nvalid, act, w_down, b_down.reshape(n_exp, 1, d))


def _combine_body(pstart_ref, eidx_ref, rank_ref, gate_ref, x2_ref, rows_ref, o_ref, buf, sem,
                  *, tokens):
    n_assign = tokens * TOP_K

    def gather(a):
        src = pstart_ref[eidx_ref[a]] + rank_ref[a]
        return pltpu.make_async_copy(rows_ref.at[pl.ds(src, 1)],
                                     buf.at[a & (TOP_K - 1), pl.ds(a >> TOP_K_SHIFT, 1)], sem)

    def start(a, carry):
        gather(a).start()
        return carry

    def wait(a, carry):
        gather(a).wait()
        return carry

    lax.fori_loop(0, n_assign, start, 0)
    lax.fori_loop(0, n_assign, wait, 0)

    y = x2_ref[...]
    gate = gate_ref[...]
    for r in range(TOP_K):
        y = y + gate[:, r:r + 1] * buf[r]
    o_ref[...] = y


def _combine(x2, rows, eidx, rank, gate, pstart):
    t, d = x2.shape
    tokens = _pick(t, (256, 128))
    smem = lambda: pl.BlockSpec((tokens * TOP_K,), lambda i, *_: (i,), memory_space=pltpu.SMEM)
    return pl.pallas_call(
        functools.partial(_combine_body, tokens=tokens),
        grid_spec=pltpu.PrefetchScalarGridSpec(
            num_scalar_prefetch=1,
            grid=(t // tokens,),
            in_specs=[smem(), smem(),
                      pl.BlockSpec((tokens, V7X_LANES), lambda i, *_: (i, 0)),
                      pl.BlockSpec((tokens, d), lambda i, *_: (i, 0)),
                      pl.BlockSpec(memory_space=pl.ANY)],
            out_specs=pl.BlockSpec((tokens, d), lambda i, *_: (i, 0)),
            scratch_shapes=[pltpu.VMEM((TOP_K, tokens, d), _F32), pltpu.SemaphoreType.DMA],
        ),
        out_shape=jax.ShapeDtypeStruct((t, d), _F32),
        compiler_params=_params("arbitrary"),
        name="combine",
    )(pstart, eidx, rank, gate, x2, rows)


def _pad_lanes(v, fill=0.0):
    v = v.reshape(1, -1).astype(_F32)
    return jnp.pad(v, ((0, 0), (0, V7X_LANES - v.shape[1])), constant_values=fill)


def kernel(x, lb_logits, norm1, w_in, hg_norm, fox_q_norm, fox_k_norm, fox_f_bias, w_up_hg, w_up_fox,
           w_out, norm2, router_w, router_b, w_gu, b_gu, w_down, b_down):
    batch, seq, d = x.shape
    t = batch * seq
    depth = w_in.shape[0]
    hd = hg_norm.shape[1]
    hg_heads = w_up_hg.shape[1] // hd
    fox_heads = fox_f_bias.shape[1]
    hg_w, fox_w = hg_heads * hd, fox_heads * hd
    n_exp = router_w.shape[2]
    assert hd == V7X_LANES and fox_q_norm.shape[1] == hd and lb_logits.shape[1] == hg_w
    assert fox_heads <= V7X_SUBLANES and n_exp <= V7X_LANES and d % V7X_LANES == 0
    assert w_in.shape[2] == 4 * hg_w + 3 * fox_w + fox_heads + 2 * d

    ff0 = 4 * hg_w + 3 * fox_w
    slab_ga = 0
    slab_gb = d // V7X_LANES
    slab_hq = 2 * (d // V7X_LANES)
    slab_fq = slab_hq + 4 * hg_heads

    n_rows = t * TOP_K + n_exp * EXPERT_ROWS
    zeros = jnp.zeros((EXPERT_ROWS, d), _F32)
    xf = x.reshape(t, d)
    for l in range(depth):
        w = w_in[l]
        w_main = jnp.concatenate([w[:, ff0 + fox_heads:], w[:, :ff0]], axis=1).astype(_BF16)
        w_ff = jnp.pad(w[:, ff0:ff0 + fox_heads], ((0, 0), (0, V7X_LANES - fox_heads)))
        p, ff = _in_proj(xf, norm1[l].reshape(1, d), w_main, w_ff)

        a = _hgrn2(p, lb_logits, hg_norm[l].reshape(1, hd), layer=l, batch=batch, seq=seq,
                   heads=hg_heads, slab_q=slab_hq, slab_f=slab_hq + hg_heads,
                   slab_i=slab_hq + 2 * hg_heads, slab_g=slab_hq + 3 * hg_heads)

        c_col, c_row = _fox_decay(ff, _pad_lanes(fox_f_bias[l]), batch=batch, seq=seq,
                                  heads_pad=V7X_SUBLANES)
        tq = _pick(seq, (512, 256, 128))
        c_row = c_row[:, :fox_heads, :].reshape(batch * fox_heads, seq // tq, tq)
        b = _fox_attn(p, c_col, c_row, fox_q_norm[l].reshape(1, hd), fox_k_norm[l].reshape(1, hd),
                      batch=batch, seq=seq, heads=fox_heads, slab_q=slab_fq,
                      slab_k=slab_fq + fox_heads, slab_v=slab_fq + 2 * fox_heads)

        rw_pad = jnp.pad(router_w[l], ((0, 0), (0, V7X_LANES - n_exp)))
        x2, h2, route, gate, counts = _merge(
            a, b, p, xf, w_up_hg[l].astype(_BF16), w_up_fox[l].astype(_BF16),
            w_out[l].astype(_BF16), norm2[l].reshape(1, d), rw_pad,
            _pad_lanes(router_b[l], NEG_BIG), slab_ga=slab_ga, slab_gb=slab_gb)

        counts = counts[0, :n_exp]
        padded = (counts + EXPERT_ROWS - 1) // EXPERT_ROWS * EXPERT_ROWS
        pend = jnp.cumsum(padded)
        pstart = jnp.concatenate([jnp.zeros((1,), _I32), pend]).astype(_I32)
        n_blocks = n_rows // EXPERT_ROWS
        bexp = jnp.minimum(
            jnp.searchsorted(pend, jnp.arange(n_blocks, dtype=_I32) * EXPERT_ROWS, side="right"),
            n_exp - 1).astype(_I32)
        nvalid = (pend[-1:] // EXPERT_ROWS).astype(_I32)
        eidx = route[:, :TOP_K].reshape(-1)
        rank = route[:, TOP_K:2 * TOP_K].reshape(-1)

        xs = _dispatch(h2, eidx, rank, pstart, counts, zeros, n_rows=n_rows)
        act = _expert_gu(xs, w_gu[l], b_gu[l], bexp, nvalid)
        rows = _expert_down(act, w_down[l], b_down[l], bexp, nvalid)
        xf = _combine(x2, rows, eidx, rank, gate, pstart)
    return xf.reshape(batch, seq, d)
```

```python
import functools

import jax
import jax.numpy as jnp
from jax import lax
from jax.experimental import pallas as pl
from jax.experimental.pallas import tpu as pltpu

_F32 = jnp.float32
_BF16 = jnp.bfloat16
_I32 = jnp.int32
_U32 = jnp.uint32

V7X_LANES = 128
V7X_SUBLANES = 8
V7X_VMEM_BYTES = 64 * 1024 * 1024
VMEM_LIMIT_BYTES = V7X_VMEM_BYTES * 7 // 8

EPS = 1e-6
SWIGLU_LIMIT = 7.0
SWIGLU_ALPHA = 1.702
TOP_K = 4
TOP_K_SHIFT = TOP_K.bit_length() - 1
assert 1 << TOP_K_SHIFT == TOP_K
NEG_BIG = -1e30
LOG2E = 1.4426950408889634

HGRN_ROWS = 256
HGRN_DIAG = V7X_SUBLANES
FOX_SUB = 256
EXPERT_ROWS = 512
DMA_UNROLL = 8


def _params(*semantics):
    return pltpu.CompilerParams(dimension_semantics=semantics, vmem_limit_bytes=VMEM_LIMIT_BYTES)


def _pick(n, candidates):
    for c in candidates:
        if n % c == 0:
            return c
    raise ValueError(f"no tile in {candidates} divides {n}")


def _sigmoid(x):
    return 1.0 / (1.0 + jnp.exp(-x))


def _split3(x):
    hi = x.astype(_BF16).astype(_F32)
    r = x - hi
    mid = r.astype(_BF16).astype(_F32)
    lo = (r - mid).astype(_BF16).astype(_F32)
    return hi, mid, lo


def _dot_exact_lhs(lhs_bf16, x):
    out = None
    for part in _split3(x):
        t = jnp.dot(lhs_bf16, part.astype(_BF16), preferred_element_type=_F32)
        out = t if out is None else out + t
    return out


def _dot_f32(a, b):
    a_hi, a_lo, _ = _split3(a)
    b_hi, b_lo, _ = _split3(b)
    a_hi, a_lo, b_hi, b_lo = (t.astype(_BF16) for t in (a_hi, a_lo, b_hi, b_lo))
    return (jnp.dot(a_hi, b_hi, preferred_element_type=_F32)
            + jnp.dot(a_hi, b_lo, preferred_element_type=_F32)
            + jnp.dot(a_lo, b_hi, preferred_element_type=_F32))


def _rms(x, gain):
    return x * lax.rsqrt(jnp.mean(x * x, axis=-1, keepdims=True) + EPS) * gain


def _pack_pair(lo, hi):
    lo_bits = lax.bitcast_convert_type(lo.astype(_BF16).astype(_F32), _U32)
    hi_bits = lax.bitcast_convert_type(hi.astype(_BF16).astype(_F32), _U32)
    return (lo_bits >> 16) | (hi_bits & jnp.uint32(0xFFFF0000))


def _unpack_pair(words):
    lo = lax.bitcast_convert_type(words << 16, _F32)
    hi = lax.bitcast_convert_type(words & jnp.uint32(0xFFFF0000), _F32)
    return lo, hi


def _store_token_tiles(ref, x, rpt):
    rows = x.shape[0]
    for k in range(rpt):
        lo = x[:, (2 * k) * V7X_LANES:(2 * k + 1) * V7X_LANES]
        hi = x[:, (2 * k + 1) * V7X_LANES:(2 * k + 2) * V7X_LANES]
        ref[pl.ds(k, rows, stride=rpt), :] = _pack_pair(lo, hi)


def _load_token_tiles(ref, rows, rpt, first=0):
    slabs = []
    for k in range(rpt):
        slabs.extend(_unpack_pair(ref[pl.ds(first + k, rows, stride=rpt), :]))
    return slabs


def _inproj_body(x_ref, g_ref, w_ref, wff_ref, p_ref, ff_ref, h_scr):
    @pl.when(pl.program_id(1) == 0)
    def _():
        h = _rms(x_ref[...], g_ref[...])
        h_scr[...] = h.astype(_BF16)
        ff_ref[...] = _dot_f32(h, wff_ref[...])

    acc = jnp.dot(h_scr[...], w_ref[...], preferred_element_type=_F32)
    for s in range(p_ref.shape[0]):
        p_ref[s] = acc[:, s * V7X_LANES:(s + 1) * V7X_LANES].astype(_BF16)


def _in_proj(x2d, gain, w_main, w_ff):
    t, d = x2d.shape
    n = w_main.shape[1]
    tm = _pick(t, (1024, 512, 256, 128))
    tn = _pick(n, (1024, 512, 256, 128))
    slabs = tn // V7X_LANES
    return pl.pallas_call(
        _inproj_body,
        grid=(t // tm, n // tn),
        in_specs=[
            pl.BlockSpec((tm, d), lambda i, j: (i, 0)),
            pl.BlockSpec((1, d), lambda i, j: (0, 0)),
            pl.BlockSpec((d, tn), lambda i, j: (0, j)),
            pl.BlockSpec((d, V7X_LANES), lambda i, j: (0, 0)),
        ],
        out_specs=[
            pl.BlockSpec((slabs, tm, V7X_LANES), lambda i, j: (j, i, 0)),
            pl.BlockSpec((tm, V7X_LANES), lambda i, j: (i, 0)),
        ],
        out_shape=[
            jax.ShapeDtypeStruct((n // V7X_LANES, t, V7X_LANES), _BF16),
            jax.ShapeDtypeStruct((t, V7X_LANES), _F32),
        ],
        scratch_shapes=[pltpu.VMEM((tm, d), _BF16)],
        compiler_params=_params("arbitrary", "arbitrary"),
        name="in_proj",
    )(x2d, gain, w_main, w_ff)


def _hgrn_body(lbl_ref, q_ref, f_ref, i_ref, g_ref, gain_ref, o_ref,
               st_ref, b_scr, q_scr, k_scr, d_scr, *, layer, rows, dk):
    @pl.when(pl.program_id(2) == 0)
    def _():
        st_ref[...] = jnp.zeros_like(st_ref)

    lbl = lbl_ref[...]
    e = jnp.exp(lbl - jnp.max(lbl, axis=0, keepdims=True))
    lb = jnp.sum(e[:layer + 1], axis=0, keepdims=True) / jnp.sum(e, axis=0, keepdims=True)

    q = q_ref[0].astype(_F32)
    qf = q * _sigmoid(q) * (dk ** -0.5)
    fg = lb + (1.0 - lb) * _sigmoid(f_ref[0].astype(_F32))
    k = 1.0 - fg
    v = i_ref[0]

    row = lax.broadcasted_iota(_I32, (rows, rows), 0)
    col = lax.broadcasted_iota(_I32, (rows, rows), 1)
    b = _dot_exact_lhs((row >= col).astype(_BF16), jnp.log(fg))

    b_scr[...] = b
    q_scr[...] = qf
    k_scr[...] = k

    def diag_block(gi, carry):
        r0 = pl.multiple_of(gi * HGRN_DIAG, HGRN_DIAG)
        bb = b_scr[pl.ds(r0, HGRN_DIAG), :]
        qq = q_scr[pl.ds(r0, HGRN_DIAG), :]
        kk = k_scr[pl.ds(r0, HGRN_DIAG), :]
        sub = lax.broadcasted_iota(_I32, bb.shape, 0)
        colb = lax.broadcasted_iota(_I32, (HGRN_DIAG, rows), 1)
        acc = jnp.zeros((HGRN_DIAG, rows), _F32)
        for sl in range(HGRN_DIAG):
            decay = jnp.exp(jnp.where(sub >= sl, bb - bb[sl:sl + 1, :], NEG_BIG))
            sc = jnp.sum(qq * decay * kk[sl:sl + 1, :], axis=-1, keepdims=True)
            acc = jnp.where(colb == r0 + sl, sc, acc)
        d_scr[pl.ds(r0, HGRN_DIAG), :] = acc
        return carry

    lax.fori_loop(0, rows // HGRN_DIAG, diag_block, 0, unroll=8)

    scores = d_scr[...]
    rowv = lax.broadcasted_iota(_I32, (rows, V7X_LANES), 0)
    x = rows // 2
    while x >= HGRN_DIAG:
        mids = [jnp.broadcast_to(b[m + x - 1:m + x, :], (2 * x, b.shape[1]))
                for m in range(0, rows, 2 * x)]
        bm = mids[0] if len(mids) == 1 else jnp.concatenate(mids, axis=0)
        upper = (rowv & x) != 0
        w = jnp.exp(-jnp.abs(b - bm))
        qx = jnp.where(upper, qf * w, 0.0).astype(_BF16)
        kx = jnp.where(upper, 0.0, k * w).astype(_BF16)
        sx = lax.dot_general(qx, kx, (((1,), (1,)), ((), ())), preferred_element_type=_F32)
        if 2 * x < rows:
            shift = (2 * x).bit_length() - 1
            sx = jnp.where((row >> shift) == (col >> shift), sx, 0.0)
        scores = scores + sx
        x //= 2

    o = jnp.dot(scores.astype(_BF16), v, preferred_element_type=_F32)

    st = st_ref[...]
    q_in = (qf * jnp.exp(b)).astype(_BF16)
    o = o + lax.dot_general(q_in, st.astype(_BF16), (((1,), (1,)), ((), ())),
                            preferred_element_type=_F32)
    b_last = b[rows - 1:rows, :]
    k_out = (k * jnp.exp(b_last - b)).astype(_BF16)
    upd = lax.dot_general(v, k_out, (((0,), (0,)), ((), ())), preferred_element_type=_F32)
    st_ref[...] = st * jnp.exp(b_last) + upd

    g = g_ref[0].astype(_F32)
    o_ref[...] = (_rms(o, gain_ref[...]) * (g * _sigmoid(g))).astype(o_ref.dtype)


def _hgrn2(p, lb_logits, gain, *, layer, batch, seq, heads, slab_q, slab_f, slab_i, slab_g):
    t = batch * seq
    hd = p.shape[2]
    rows = min(HGRN_ROWS, seq)
    nblk = seq // rows

    def slab(first):
        return pl.BlockSpec((1, rows, hd), lambda b, h, n: (first + h, b * nblk + n, 0))

    return pl.pallas_call(
        functools.partial(_hgrn_body, layer=layer, rows=rows, dk=hd),
        grid=(batch, heads, nblk),
        in_specs=[
            pl.BlockSpec((lb_logits.shape[0], hd), lambda b, h, n: (0, h)),
            slab(slab_q), slab(slab_f), slab(slab_i), slab(slab_g),
            pl.BlockSpec((1, hd), lambda b, h, n: (0, 0)),
        ],
        out_specs=pl.BlockSpec((rows, hd), lambda b, h, n: (b * nblk + n, h)),
        out_shape=jax.ShapeDtypeStruct((t, heads * hd), _BF16),
        scratch_shapes=[
            pltpu.VMEM((hd, hd), _F32),
            pltpu.VMEM((rows, hd), _F32),
            pltpu.VMEM((rows, hd), _F32),
            pltpu.VMEM((rows, hd), _F32),
            pltpu.VMEM((rows, rows), _F32),
        ],
        compiler_params=_params("arbitrary", "arbitrary", "arbitrary"),
        name="hgrn2",
    )(lb_logits, p, p, p, p, gain)


def _fox_decay_body(ff_ref, bias_ref, c_ref, carry_ref):
    @pl.when(pl.program_id(1) == 0)
    def _():
        carry_ref[...] = jnp.zeros_like(carry_ref)

    x = ff_ref[...] + bias_ref[...]
    logsig = jnp.minimum(x, 0.0) - jnp.log(1.0 + jnp.exp(-jnp.abs(x)))
    rows = x.shape[0]
    row = lax.broadcasted_iota(_I32, (rows, rows), 0)
    col = lax.broadcasted_iota(_I32, (rows, rows), 1)
    c = _dot_exact_lhs((row >= col).astype(_BF16), logsig) + carry_ref[...]
    c_ref[...] = c
    carry_ref[...] = c[rows - 1:rows, :]


def _fox_decay(ff, bias_pad, *, batch, seq):
    t = batch * seq
    rows = _pick(seq, (512, 256, 128))
    nblk = seq // rows
    return pl.pallas_call(
        _fox_decay_body,
        grid=(batch, nblk),
        in_specs=[
            pl.BlockSpec((rows, V7X_LANES), lambda b, n: (b * nblk + n, 0)),
            pl.BlockSpec((1, V7X_LANES), lambda b, n: (0, 0)),
        ],
        out_specs=pl.BlockSpec((rows, V7X_LANES), lambda b, n: (b * nblk + n, 0)),
        out_shape=jax.ShapeDtypeStruct((t, V7X_LANES), _F32),
        scratch_shapes=[pltpu.VMEM((1, V7X_LANES), _F32)],
        compiler_params=_params("arbitrary", "arbitrary"),
        name="fox_decay",
    )(ff, bias_pad)


def _fox_body(q_ref, k_ref, v_ref, c_ref, qg_ref, kg_ref, o_ref,
              ka_scr, qa_scr, m_scr, l_scr, acc_scr, *, tq, seq):
    h = pl.program_id(1)
    qi = pl.program_id(2)
    hd = q_ref.shape[2]
    lane = lax.broadcasted_iota(_I32, (tq, V7X_LANES), 1)

    def bias_lanes(c_block, sign, parts_first):
        c = jnp.sum(jnp.where(lane == h, c_block, 0.0), axis=-1, keepdims=True) * (sign * LOG2E)
        parts_at = 0 if parts_first else 3
        ones_at = 3 if parts_first else 0
        x = jnp.where((lane >= ones_at) & (lane < ones_at + 3), 1.0, 0.0)
        for j, part in enumerate(_split3(c)):
            x = jnp.where(lane == parts_at + j, part, x)
        return x.astype(_BF16)

    @pl.when(qi == 0)
    def _():
        def key_block(c, carry):
            r0 = pl.multiple_of(c * tq, tq)
            kk = k_ref[0, pl.ds(r0, tq), :].astype(_F32)
            ka_scr[pl.ds(r0, tq), :hd] = _rms(kk, kg_ref[...]).astype(_BF16)
            ka_scr[pl.ds(r0, tq), hd:] = bias_lanes(c_ref[pl.ds(r0, tq), :], -1.0, True)
            return carry
        lax.fori_loop(0, seq // tq, key_block, 0)

    q0 = pl.multiple_of(qi * tq, tq)
    qn = _rms(q_ref[0].astype(_F32), qg_ref[...]) * ((hd ** -0.5) * LOG2E)
    qa_scr[:, :hd] = qn.astype(_BF16)
    qa_scr[:, hd:] = bias_lanes(c_ref[pl.ds(q0, tq), :], 1.0, False)

    m_scr[...] = jnp.full_like(m_scr, NEG_BIG)
    l_scr[...] = jnp.zeros_like(l_scr)
    acc_scr[...] = jnp.zeros_like(acc_scr)

    def step(j, diagonal):
        r0 = pl.multiple_of(j * tq, tq)
        for u in range(tq // FOX_SUB):
            rows = slice(u * FOX_SUB, (u + 1) * FOX_SUB)
            nk = (u + 1) * FOX_SUB if diagonal else tq
            s = lax.dot_general(qa_scr[rows, :], ka_scr[pl.ds(r0, nk), :],
                                (((1,), (1,)), ((), ())), preferred_element_type=_F32)
            if diagonal:
                row = lax.broadcasted_iota(_I32, s.shape, 0) + u * FOX_SUB
                col = lax.broadcasted_iota(_I32, s.shape, 1)
                s = jnp.where(col <= row, s, NEG_BIG)
            m_old = m_scr[rows, :]
            m_new = jnp.maximum(m_old, jnp.max(s, axis=-1, keepdims=True))
            alpha = jnp.exp2(m_old - m_new)
            p = jnp.exp2(s - jnp.tile(m_new, (1, nk // V7X_LANES)))
            l_scr[rows, :] = alpha * l_scr[rows, :] + jnp.sum(p, axis=-1, keepdims=True)
            acc_scr[rows, :] = alpha * acc_scr[rows, :] + jnp.dot(
                p.astype(_BF16), v_ref[0, pl.ds(r0, nk), :], preferred_element_type=_F32)
            m_scr[rows, :] = m_new

    def full_step(j, carry):
        step(j, False)
        return carry

    lax.fori_loop(0, qi, full_step, 0)
    step(qi, True)
    o_ref[...] = (acc_scr[...] / l_scr[...]).astype(o_ref.dtype)


def _fox_attn(p, c, q_gain, k_gain, *, batch, seq, heads, slab_q, slab_k, slab_v):
    t = batch * seq
    hd = p.shape[2]
    tq = _pick(seq, (512, 256))
    nq = seq // tq
    return pl.pallas_call(
        functools.partial(_fox_body, tq=tq, seq=seq),
        grid=(batch, heads, nq),
        in_specs=[
            pl.BlockSpec((1, tq, hd), lambda b, h, i: (slab_q + h, b * nq + i, 0)),
            pl.BlockSpec((1, seq, hd), lambda b, h, i: (slab_k + h, b, 0)),
            pl.BlockSpec((1, seq, hd), lambda b, h, i: (slab_v + h, b, 0)),
            pl.BlockSpec((seq, V7X_LANES), lambda b, h, i: (b, 0)),
            pl.BlockSpec((1, hd), lambda b, h, i: (0, 0)),
            pl.BlockSpec((1, hd), lambda b, h, i: (0, 0)),
        ],
        out_specs=pl.BlockSpec((tq, hd), lambda b, h, i: (b * nq + i, h)),
        out_shape=jax.ShapeDtypeStruct((t, heads * hd), _BF16),
        scratch_shapes=[
            pltpu.VMEM((seq, 2 * hd), _BF16),
            pltpu.VMEM((tq, 2 * hd), _BF16),
            pltpu.VMEM((tq, V7X_LANES), _F32),
            pltpu.VMEM((tq, V7X_LANES), _F32),
            pltpu.VMEM((tq, hd), _F32),
        ],
        compiler_params=_params("arbitrary", "arbitrary", "arbitrary"),
        name="fox_attn",
    )(p, p, p, c, q_gain, k_gain)


def _merge_body(a_ref, b_ref, ga_ref, gb_ref, x_ref, wa_ref, wb_ref, wo_ref, n2_ref, rw_ref,
                rb_ref, x2_ref, h2_ref, route_ref, gate_ref, cnt_ref, carry_ref, *, rpt):
    @pl.when(pl.program_id(0) == 0)
    def _():
        carry_ref[...] = jnp.zeros_like(carry_ref)

    def gates(ref):
        parts = [ref[s] for s in range(ref.shape[0])]
        g = parts[0] if len(parts) == 1 else jnp.concatenate(parts, axis=-1)
        return _sigmoid(g.astype(_F32))

    ua = jnp.dot(a_ref[...], wa_ref[...], preferred_element_type=_F32)
    ub = jnp.dot(b_ref[...], wb_ref[...], preferred_element_type=_F32)
    merged = gates(ga_ref) * ua + gates(gb_ref) * ub
    x2 = x_ref[...] + jnp.dot(merged.astype(_BF16), wo_ref[...], preferred_element_type=_F32)
    x2_ref[...] = x2
    h2 = _rms(x2, n2_ref[...])
    _store_token_tiles(h2_ref, h2, rpt)

    logits = _dot_f32(h2, rw_ref[...]) + rb_ref[...]
    tm = logits.shape[0]
    lane = lax.broadcasted_iota(_I32, logits.shape, 1)
    lane_f = lane.astype(_F32)
    work = logits
    top_v, top_i = [], []
    for _ in range(TOP_K):
        mx = jnp.max(work, axis=-1, keepdims=True)
        ix = jnp.min(jnp.where(work == mx, lane_f, float(V7X_LANES)), axis=-1,
                     keepdims=True).astype(_I32)
        top_v.append(mx)
        top_i.append(ix)
        work = jnp.where(lane == ix, NEG_BIG, work)

    ex = [jnp.exp(tv - top_v[0]) for tv in top_v]
    denom = ex[0]
    for t_ in ex[1:]:
        denom = denom + t_

    onehot = jnp.zeros(logits.shape, _F32)
    for ix in top_i:
        onehot = onehot + (lane == ix).astype(_F32)
    row = lax.broadcasted_iota(_I32, (tm, tm), 0)
    col = lax.broadcasted_iota(_I32, (tm, tm), 1)
    before = jnp.dot((row > col).astype(_BF16), onehot.astype(_BF16),
                     preferred_element_type=_F32) + carry_ref[...]

    route = jnp.zeros(logits.shape, _I32)
    gate = jnp.zeros(logits.shape, _F32)
    for r in range(TOP_K):
        rank = jnp.sum(jnp.where(lane == top_i[r], before, 0.0), axis=-1, keepdims=True)
        route = jnp.where(lane == r, top_i[r], route)
        route = jnp.where(lane == TOP_K + r, rank.astype(_I32), route)
        gate = jnp.where(lane == r, ex[r] / denom, gate)
    route_ref[...] = route
    gate_ref[...] = gate

    carry = carry_ref[...] + jnp.sum(onehot, axis=0, keepdims=True)
    carry_ref[...] = carry
    cnt_ref[...] = carry.astype(_I32)


def _merge(a, b, p, x2d, wa, wb, wo, n2, rw_pad, rb_pad, *, slab_ga, slab_gb, rpt):
    t, d = x2d.shape
    tm = _pick(t, (256, 128))
    gs = d // V7X_LANES
    const = lambda i: (0, 0)
    once = pl.Buffered(1)
    return pl.pallas_call(
        functools.partial(_merge_body, rpt=rpt),
        grid=(t // tm,),
        in_specs=[
            pl.BlockSpec((tm, a.shape[1]), lambda i: (i, 0)),
            pl.BlockSpec((tm, b.shape[1]), lambda i: (i, 0)),
            pl.BlockSpec((gs, tm, V7X_LANES), lambda i: (slab_ga // gs, i, 0)),
            pl.BlockSpec((gs, tm, V7X_LANES), lambda i: (slab_gb // gs, i, 0)),
            pl.BlockSpec((tm, d), lambda i: (i, 0)),
            pl.BlockSpec(wa.shape, const, pipeline_mode=once),
            pl.BlockSpec(wb.shape, const, pipeline_mode=once),
            pl.BlockSpec(wo.shape, const, pipeline_mode=once),
            pl.BlockSpec((1, d), const),
            pl.BlockSpec(rw_pad.shape, const, pipeline_mode=once),
            pl.BlockSpec((1, V7X_LANES), const),
        ],
        out_specs=[
            pl.BlockSpec((tm, d), lambda i: (i, 0)),
            pl.BlockSpec((tm * rpt, V7X_LANES), lambda i: (i, 0)),
            pl.BlockSpec((tm, V7X_LANES), lambda i: (i, 0)),
            pl.BlockSpec((tm, V7X_LANES), lambda i: (i, 0)),
            pl.BlockSpec((1, V7X_LANES), const),
        ],
        out_shape=[
            jax.ShapeDtypeStruct((t, d), _F32),
            jax.ShapeDtypeStruct((t * rpt, V7X_LANES), _U32),
            jax.ShapeDtypeStruct((t, V7X_LANES), _I32),
            jax.ShapeDtypeStruct((t, V7X_LANES), _F32),
            jax.ShapeDtypeStruct((1, V7X_LANES), _I32),
        ],
        scratch_shapes=[pltpu.VMEM((1, V7X_LANES), _F32)],
        compiler_params=_params("arbitrary"),
        name="merge_router",
    )(a, b, p, p, x2d, wa, wb, wo, n2, rw_pad, rb_pad)


def _tile_copy(src, src_tile, dst, dst_tile, sem, rpt, tiles=1):
    def first_row(tile):
        return tile * rpt if isinstance(tile, int) else pl.multiple_of(tile * rpt, rpt)

    n = tiles * rpt
    return pltpu.make_async_copy(src.at[pl.ds(first_row(src_tile), n)],
                                 dst.at[pl.ds(first_row(dst_tile), n)], sem)


def _dispatch_body(pstart_ref, cnt_ref, eidx_ref, rank_ref, h2_ref, xs_ref, zeros_scr,
                   sem, zsem, *, tokens, n_exp, n_rows, rpt):
    step = pl.program_id(0)
    n_assign = tokens * TOP_K

    def token_copy(a):
        dst = pstart_ref[eidx_ref[a]] + rank_ref[a]
        return _tile_copy(h2_ref, a >> TOP_K_SHIFT, xs_ref, dst, sem, rpt)

    def start(a, carry):
        token_copy(a).start()
        return carry

    def wait(a, carry):
        token_copy(a).wait()
        return carry

    def fill(do):
        def expert_pad(e, carry):
            first = pstart_ref[e] + cnt_ref[e]

            def pad_tile(c, carry):
                do(_tile_copy(zeros_scr, 0, xs_ref, first + c, zsem, rpt))
                return carry

            lax.fori_loop(0, pstart_ref[e + 1] - first, pad_tile, 0)
            return carry

        lax.fori_loop(0, n_exp, expert_pad, 0)

        tail = pstart_ref[n_exp]

        def tail_block(c, carry):
            do(_tile_copy(zeros_scr, 0, xs_ref, tail + c * EXPERT_ROWS, zsem, rpt, EXPERT_ROWS))
            return carry

        lax.fori_loop(0, (n_rows - tail) // EXPERT_ROWS, tail_block, 0)

    @pl.when(step == 0)
    def _():
        zeros_scr[...] = jnp.zeros_like(zeros_scr)
        fill(lambda c: c.start())

    lax.fori_loop(0, n_assign, start, 0, unroll=DMA_UNROLL)
    lax.fori_loop(0, n_assign, wait, 0, unroll=DMA_UNROLL)

    @pl.when(step == 0)
    def _():
        fill(lambda c: c.wait())


def _dispatch(h2, eidx, rank, pstart, counts, *, n_rows, rpt):
    t = h2.shape[0] // rpt
    tokens = _pick(t, (256, 128))
    n_exp = counts.shape[0]
    smem = lambda: pl.BlockSpec((tokens * TOP_K,), lambda i, *_: (i,), memory_space=pltpu.SMEM)
    return pl.pallas_call(
        functools.partial(_dispatch_body, tokens=tokens, n_exp=n_exp, n_rows=n_rows, rpt=rpt),
        grid_spec=pltpu.PrefetchScalarGridSpec(
            num_scalar_prefetch=2,
            grid=(t // tokens,),
            in_specs=[smem(), smem(),
                      pl.BlockSpec((tokens * rpt, V7X_LANES), lambda i, *_: (i, 0))],
            out_specs=pl.BlockSpec(memory_space=pl.ANY),
            scratch_shapes=[pltpu.VMEM((EXPERT_ROWS * rpt, V7X_LANES), _U32),
                            pltpu.SemaphoreType.DMA, pltpu.SemaphoreType.DMA],
        ),
        out_shape=jax.ShapeDtypeStruct((n_rows * rpt, V7X_LANES), _U32),
        compiler_params=_params("arbitrary"),
        name="dispatch",
    )(pstart, counts, eidx, rank, h2)


def _expert_block_state(bexp_ref, nvalid_ref):
    i = pl.program_id(1)
    valid = i < nvalid_ref[0]
    changed = jnp.logical_or(i == 0, bexp_ref[i] != bexp_ref[jnp.maximum(i - 1, 0)])
    return valid, jnp.logical_and(valid, changed)


def _expert_gu_body(bexp_ref, nvalid_ref, x_ref, wg_ref, wu_ref, bg_ref, bu_ref, o_ref, w_scr,
                    *, rpt):
    valid, changed = _expert_block_state(bexp_ref, nvalid_ref)

    @pl.when(changed)
    def _():
        w_scr[0] = wg_ref[0].astype(_BF16)
        w_scr[1] = wu_ref[0].astype(_BF16)

    @pl.when(valid)
    def _():
        slabs = _load_token_tiles(x_ref, o_ref.shape[0], rpt)
        x = jnp.concatenate([s.astype(_BF16) for s in slabs], axis=-1)
        gate = jnp.dot(x, w_scr[0], preferred_element_type=_F32) + bg_ref[0]
        up = jnp.dot(x, w_scr[1], preferred_element_type=_F32) + bu_ref[0]
        gate = jnp.minimum(gate, SWIGLU_LIMIT)
        up = jnp.clip(up, -SWIGLU_LIMIT, SWIGLU_LIMIT)
        o_ref[...] = ((up + 1.0) * gate * _sigmoid(SWIGLU_ALPHA * gate)).astype(o_ref.dtype)

    @pl.when(jnp.logical_not(valid))
    def _():
        o_ref[...] = jnp.zeros_like(o_ref)


def _block_index(i, nvalid_ref):
    return jnp.minimum(i, nvalid_ref[0] - 1)


def _expert_gu(xs, w_gu, b_gu, bexp, nvalid, *, rpt):
    n_rows = xs.shape[0] // rpt
    n_exp, d, two_de = w_gu.shape
    de = two_de // 2
    tm = EXPERT_ROWS
    tn = _pick(de, (512, 256, 128))
    nj = de // tn
    blk = _block_index
    return pl.pallas_call(
        functools.partial(_expert_gu_body, rpt=rpt),
        grid_spec=pltpu.PrefetchScalarGridSpec(
            num_scalar_prefetch=2,
            grid=(nj, n_rows // tm),
            in_specs=[
                pl.BlockSpec((tm * rpt, V7X_LANES), lambda j, i, be, nv: (blk(i, nv), 0)),
                pl.BlockSpec((1, d, tn), lambda j, i, be, nv: (be[blk(i, nv)], 0, j)),
                pl.BlockSpec((1, d, tn), lambda j, i, be, nv: (be[blk(i, nv)], 0, nj + j)),
                pl.BlockSpec((1, 1, tn), lambda j, i, be, nv: (be[blk(i, nv)], 0, j)),
                pl.BlockSpec((1, 1, tn), lambda j, i, be, nv: (be[blk(i, nv)], 0, nj + j)),
            ],
            out_specs=pl.BlockSpec((tm, tn), lambda j, i, be, nv: (i, j)),
            scratch_shapes=[pltpu.VMEM((2, d, tn), _BF16)],
        ),
        out_shape=jax.ShapeDtypeStruct((n_rows, de), _BF16),
        compiler_params=_params("arbitrary", "arbitrary"),
        name="expert_gu",
    )(bexp, nvalid, xs, w_gu, w_gu, b_gu.reshape(n_exp, 1, two_de), b_gu.reshape(n_exp, 1, two_de))


def _expert_down_body(bexp_ref, nvalid_ref, a_ref, w_ref, b_ref, o_ref, w_scr, *, rpt):
    valid, changed = _expert_block_state(bexp_ref, nvalid_ref)

    @pl.when(changed)
    def _():
        w_scr[...] = w_ref[0].astype(_BF16)

    @pl.when(valid)
    def _():
        out = jnp.dot(a_ref[...], w_scr[...], preferred_element_type=_F32) + b_ref[0]
        _store_token_tiles(o_ref, out, rpt)

    @pl.when(jnp.logical_not(valid))
    def _():
        o_ref[...] = jnp.zeros_like(o_ref)


def _expert_down(act, w_down, b_down, bexp, nvalid, *, rpt):
    n_rows, de = act.shape
    n_exp, _, d = w_down.shape
    tm = EXPERT_ROWS
    blk = _block_index
    return pl.pallas_call(
        functools.partial(_expert_down_body, rpt=rpt),
        grid_spec=pltpu.PrefetchScalarGridSpec(
            num_scalar_prefetch=2,
            grid=(1, n_rows // tm),
            in_specs=[
                pl.BlockSpec((tm, de), lambda j, i, be, nv: (blk(i, nv), 0)),
                pl.BlockSpec((1, de, d), lambda j, i, be, nv: (be[blk(i, nv)], 0, 0),
                             pipeline_mode=pl.Buffered(1)),
                pl.BlockSpec((1, 1, d), lambda j, i, be, nv: (be[blk(i, nv)], 0, 0)),
            ],
            out_specs=pl.BlockSpec((tm * rpt, V7X_LANES), lambda j, i, be, nv: (i, 0)),
            scratch_shapes=[pltpu.VMEM((de, d), _BF16)],
        ),
        out_shape=jax.ShapeDtypeStruct((n_rows * rpt, V7X_LANES), _U32),
        compiler_params=_params("arbitrary", "arbitrary"),
        name="expert_down",
    )(bexp, nvalid, act, w_down, b_down.reshape(n_exp, 1, d))


def _combine_body(pstart_ref, eidx_ref, rank_ref, gate_ref, x2_ref, rows_ref, o_ref, buf, sem,
                  *, tokens, rpt):
    n_assign = tokens * TOP_K

    def gather(a):
        src = pstart_ref[eidx_ref[a]] + rank_ref[a]
        slot = (a & (TOP_K - 1)) * tokens + (a >> TOP_K_SHIFT)
        return _tile_copy(rows_ref, src, buf, slot, sem, rpt)

    def start(a, carry):
        gather(a).start()
        return carry

    def wait(a, carry):
        gather(a).wait()
        return carry

    lax.fori_loop(0, n_assign, start, 0, unroll=DMA_UNROLL)
    lax.fori_loop(0, n_assign, wait, 0, unroll=DMA_UNROLL)

    gate = gate_ref[...]
    acc = [None] * (2 * rpt)
    for r in range(TOP_K):
        g = gate[:, r:r + 1]
        for s, slab in enumerate(_load_token_tiles(buf, tokens, rpt, first=r * tokens * rpt)):
            acc[s] = g * slab if acc[s] is None else acc[s] + g * slab
    o_ref[...] = x2_ref[...] + jnp.concatenate(acc, axis=-1)


def _combine(x2, rows, eidx, rank, gate, pstart, *, rpt):
    t, d = x2.shape
    tokens = _pick(t, (256, 128))
    smem = lambda: pl.BlockSpec((tokens * TOP_K,), lambda i, *_: (i,), memory_space=pltpu.SMEM)
    return pl.pallas_call(
        functools.partial(_combine_body, tokens=tokens, rpt=rpt),
        grid_spec=pltpu.PrefetchScalarGridSpec(
            num_scalar_prefetch=1,
            grid=(t // tokens,),
            in_specs=[smem(), smem(),
                      pl.BlockSpec((tokens, V7X_LANES), lambda i, *_: (i, 0)),
                      pl.BlockSpec((tokens, d), lambda i, *_: (i, 0)),
                      pl.BlockSpec(memory_space=pl.ANY)],
            out_specs=pl.BlockSpec((tokens, d), lambda i, *_: (i, 0)),
            scratch_shapes=[pltpu.VMEM((TOP_K * tokens * rpt, V7X_LANES), _U32),
                            pltpu.SemaphoreType.DMA],
        ),
        out_shape=jax.ShapeDtypeStruct((t, d), _F32),
        compiler_params=_params("arbitrary"),
        name="combine",
    )(pstart, eidx, rank, gate, x2, rows)


def _pad_lanes(v, fill=0.0):
    v = v.reshape(1, -1).astype(_F32)
    return jnp.pad(v, ((0, 0), (0, V7X_LANES - v.shape[1])), constant_values=fill)


def kernel(x, lb_logits, norm1, w_in, hg_norm, fox_q_norm, fox_k_norm, fox_f_bias, w_up_hg, w_up_fox,
           w_out, norm2, router_w, router_b, w_gu, b_gu, w_down, b_down):
    batch, seq, d = x.shape
    t = batch * seq
    depth = w_in.shape[0]
    hd = hg_norm.shape[1]
    hg_heads = w_up_hg.shape[1] // hd
    fox_heads = fox_f_bias.shape[1]
    hg_w, fox_w = hg_heads * hd, fox_heads * hd
    n_exp = router_w.shape[2]
    assert hd == V7X_LANES and fox_q_norm.shape[1] == hd and lb_logits.shape[1] == hg_w
    assert n_exp <= V7X_LANES and d % (2 * V7X_LANES) == 0
    assert w_in.shape[2] == 4 * hg_w + 3 * fox_w + fox_heads + 2 * d
    rpt = d // (2 * V7X_LANES)

    ff0 = 4 * hg_w + 3 * fox_w
    slab_ga = 0
    slab_gb = d // V7X_LANES
    slab_hq = 2 * (d // V7X_LANES)
    slab_fq = slab_hq + 4 * hg_heads

    n_rows = t * TOP_K + n_exp * EXPERT_ROWS
    xf = x.reshape(t, d)
    for l in range(depth):
        w = w_in[l]
        w_main = jnp.concatenate([w[:, ff0 + fox_heads:], w[:, :ff0]], axis=1).astype(_BF16)
        w_ff = jnp.pad(w[:, ff0:ff0 + fox_heads], ((0, 0), (0, V7X_LANES - fox_heads)))
        p, ff = _in_proj(xf, norm1[l].reshape(1, d), w_main, w_ff)

        a = _hgrn2(p, lb_logits, hg_norm[l].reshape(1, hd), layer=l, batch=batch, seq=seq,
                   heads=hg_heads, slab_q=slab_hq, slab_f=slab_hq + hg_heads,
                   slab_i=slab_hq + 2 * hg_heads, slab_g=slab_hq + 3 * hg_heads)

        c = _fox_decay(ff, _pad_lanes(fox_f_bias[l]), batch=batch, seq=seq)
        b = _fox_attn(p, c, fox_q_norm[l].reshape(1, hd), fox_k_norm[l].reshape(1, hd),
                      batch=batch, seq=seq, heads=fox_heads, slab_q=slab_fq,
                      slab_k=slab_fq + fox_heads, slab_v=slab_fq + 2 * fox_heads)

        rw_pad = jnp.pad(router_w[l], ((0, 0), (0, V7X_LANES - n_exp)))
        x2, h2, route, gate, counts = _merge(
            a, b, p, xf, w_up_hg[l].astype(_BF16), w_up_fox[l].astype(_BF16),
            w_out[l].astype(_BF16), norm2[l].reshape(1, d), rw_pad,
            _pad_lanes(router_b[l], NEG_BIG), slab_ga=slab_ga, slab_gb=slab_gb, rpt=rpt)

        counts = counts[0, :n_exp]
        padded = (counts + EXPERT_ROWS - 1) // EXPERT_ROWS * EXPERT_ROWS
        pend = jnp.cumsum(padded)
        pstart = jnp.concatenate([jnp.zeros((1,), _I32), pend]).astype(_I32)
        n_blocks = n_rows // EXPERT_ROWS
        bexp = jnp.minimum(
            jnp.searchsorted(pend, jnp.arange(n_blocks, dtype=_I32) * EXPERT_ROWS, side="right"),
            n_exp - 1).astype(_I32)
        nvalid = (pend[-1:] // EXPERT_ROWS).astype(_I32)
        eidx = route[:, :TOP_K].reshape(-1)
        rank = route[:, TOP_K:2 * TOP_K].reshape(-1)

        xs = _dispatch(h2, eidx, rank, pstart, counts, n_rows=n_rows, rpt=rpt)
        act = _expert_gu(xs, w_gu[l], b_gu[l], bexp, nvalid, rpt=rpt)
        rows = _expert_down(act, w_down[l], b_down[l], bexp, nvalid, rpt=rpt)
        xf = _combine(x2, rows, eidx, rank, gate, pstart, rpt=rpt)
    return xf.reshape(batch, seq, d)
```

```python
import functools

import jax
import jax.numpy as jnp
from jax import lax
from jax.experimental import pallas as pl
from jax.experimental.pallas import tpu as pltpu

_F32 = jnp.float32
_BF16 = jnp.bfloat16
_I32 = jnp.int32
_U32 = jnp.uint32

V7X_LANES = 128
V7X_SUBLANES = 8
V7X_VMEM_BYTES = 64 * 1024 * 1024
VMEM_LIMIT_BYTES = V7X_VMEM_BYTES * 7 // 8

EPS = 1e-6
SWIGLU_LIMIT = 7.0
SWIGLU_ALPHA = 1.702
TOP_K = 4
TOP_K_SHIFT = TOP_K.bit_length() - 1
assert 1 << TOP_K_SHIFT == TOP_K
NEG_BIG = -1e30
LOG2E = 1.4426950408889634

HGRN_ROWS = 256
HGRN_DIAG = V7X_SUBLANES
HGRN_HEADS = 2
EXPERT_ROWS = 512
DMA_UNROLL = 8


def _params(*semantics):
    return pltpu.CompilerParams(dimension_semantics=semantics, vmem_limit_bytes=VMEM_LIMIT_BYTES)


def _pick(n, candidates):
    for c in candidates:
        if n % c == 0:
            return c
    raise ValueError(f"no tile in {candidates} divides {n}")


def _sigmoid(x):
    return 1.0 / (1.0 + jnp.exp(-x))


def _split3(x):
    hi = x.astype(_BF16).astype(_F32)
    r = x - hi
    mid = r.astype(_BF16).astype(_F32)
    lo = (r - mid).astype(_BF16).astype(_F32)
    return hi, mid, lo


def _dot_exact_lhs(lhs_bf16, x):
    out = None
    for part in _split3(x):
        t = jnp.dot(lhs_bf16, part.astype(_BF16), preferred_element_type=_F32)
        out = t if out is None else out + t
    return out


def _dot_f32(a, b):
    a_hi, a_lo, _ = _split3(a)
    b_hi, b_lo, _ = _split3(b)
    a_hi, a_lo, b_hi, b_lo = (t.astype(_BF16) for t in (a_hi, a_lo, b_hi, b_lo))
    return (jnp.dot(a_hi, b_hi, preferred_element_type=_F32)
            + jnp.dot(a_hi, b_lo, preferred_element_type=_F32)
            + jnp.dot(a_lo, b_hi, preferred_element_type=_F32))


def _rms(x, gain):
    return x * lax.rsqrt(jnp.mean(x * x, axis=-1, keepdims=True) + EPS) * gain


def _pack_pair(lo, hi):
    lo_bits = lax.bitcast_convert_type(lo.astype(_BF16).astype(_F32), _U32)
    hi_bits = lax.bitcast_convert_type(hi.astype(_BF16).astype(_F32), _U32)
    return (lo_bits >> 16) | (hi_bits & jnp.uint32(0xFFFF0000))


def _unpack_pair(words):
    lo = lax.bitcast_convert_type(words << 16, _F32)
    hi = lax.bitcast_convert_type(words & jnp.uint32(0xFFFF0000), _F32)
    return lo, hi


def _store_token_tiles(ref, x, rpt):
    rows = x.shape[0]
    for k in range(rpt):
        lo = x[:, (2 * k) * V7X_LANES:(2 * k + 1) * V7X_LANES]
        hi = x[:, (2 * k + 1) * V7X_LANES:(2 * k + 2) * V7X_LANES]
        ref[pl.ds(k, rows, stride=rpt), :] = _pack_pair(lo, hi)


def _load_token_tiles(ref, rows, rpt, first=0):
    slabs = []
    for k in range(rpt):
        slabs.extend(_unpack_pair(ref[pl.ds(first + k, rows, stride=rpt), :]))
    return slabs


def _inproj_body(x_ref, g_ref, w_ref, wff_ref, p_ref, ff_ref, h_scr):
    @pl.when(pl.program_id(1) == 0)
    def _():
        h = _rms(x_ref[...], g_ref[...])
        h_scr[...] = h.astype(_BF16)
        ff_ref[...] = _dot_f32(h, wff_ref[...])

    acc = jnp.dot(h_scr[...], w_ref[...], preferred_element_type=_F32)
    for s in range(p_ref.shape[0]):
        p_ref[s] = acc[:, s * V7X_LANES:(s + 1) * V7X_LANES].astype(_BF16)


def _in_proj(x2d, gain, w_main, w_ff):
    t, d = x2d.shape
    n = w_main.shape[1]
    tm = _pick(t, (1024, 512, 256, 128))
    tn = _pick(n, (1024, 512, 256, 128))
    slabs = tn // V7X_LANES
    return pl.pallas_call(
        _inproj_body,
        grid=(t // tm, n // tn),
        in_specs=[
            pl.BlockSpec((tm, d), lambda i, j: (i, 0)),
            pl.BlockSpec((1, d), lambda i, j: (0, 0)),
            pl.BlockSpec((d, tn), lambda i, j: (0, j)),
            pl.BlockSpec((d, V7X_LANES), lambda i, j: (0, 0)),
        ],
        out_specs=[
            pl.BlockSpec((slabs, tm, V7X_LANES), lambda i, j: (j, i, 0)),
            pl.BlockSpec((tm, V7X_LANES), lambda i, j: (i, 0)),
        ],
        out_shape=[
            jax.ShapeDtypeStruct((n // V7X_LANES, t, V7X_LANES), _BF16),
            jax.ShapeDtypeStruct((t, V7X_LANES), _F32),
        ],
        scratch_shapes=[pltpu.VMEM((tm, d), _BF16)],
        compiler_params=_params("arbitrary", "arbitrary"),
        name="in_proj",
    )(x2d, gain, w_main, w_ff)


def _hgrn_body(lbl_ref, q_ref, f_ref, i_ref, g_ref, gain_ref, o_ref,
               st_ref, b_scr, q_scr, k_scr, d_scr, *, layer, rows, dk):
    @pl.when(pl.program_id(2) == 0)
    def _():
        st_ref[...] = jnp.zeros_like(st_ref)

    heads = range(q_ref.shape[0])
    for hb in heads:
        _hgrn_prepare(lbl_ref.at[:, pl.ds(hb * dk, dk)], q_ref.at[hb], f_ref.at[hb],
                      b_scr.at[hb], q_scr.at[hb], k_scr.at[hb], layer=layer, rows=rows, dk=dk)

    def diag_blocks(gi, carry):
        for hb in heads:
            _hgrn_diag_block(gi, b_scr.at[hb], q_scr.at[hb], k_scr.at[hb], d_scr.at[hb], rows)
        return carry

    lax.fori_loop(0, rows // HGRN_DIAG, diag_blocks, 0, unroll=4)

    for hb in heads:
        _hgrn_finish(i_ref.at[hb], g_ref.at[hb], gain_ref, o_ref.at[:, pl.ds(hb * dk, dk)],
                     st_ref.at[hb], b_scr.at[hb], q_scr.at[hb], k_scr.at[hb], d_scr.at[hb],
                     rows=rows)


def _hgrn_prepare(lbl_ref, q_ref, f_ref, b_scr, q_scr, k_scr, *, layer, rows, dk):
    lbl = lbl_ref[...]
    e = jnp.exp(lbl - jnp.max(lbl, axis=0, keepdims=True))
    lb = jnp.sum(e[:layer + 1], axis=0, keepdims=True) / jnp.sum(e, axis=0, keepdims=True)

    q = q_ref[...].astype(_F32)
    qf = q * _sigmoid(q) * (dk ** -0.5)
    fg = lb + (1.0 - lb) * _sigmoid(f_ref[...].astype(_F32))
    row = lax.broadcasted_iota(_I32, (rows, rows), 0)
    col = lax.broadcasted_iota(_I32, (rows, rows), 1)
    b_scr[...] = _dot_exact_lhs((row >= col).astype(_BF16), jnp.log(fg))
    q_scr[...] = qf
    k_scr[...] = 1.0 - fg


def _hgrn_diag_block(gi, b_scr, q_scr, k_scr, d_scr, rows):
    r0 = pl.multiple_of(gi * HGRN_DIAG, HGRN_DIAG)
    bb = b_scr[pl.ds(r0, HGRN_DIAG), :]
    qq = q_scr[pl.ds(r0, HGRN_DIAG), :]
    kk = k_scr[pl.ds(r0, HGRN_DIAG), :]
    sub = lax.broadcasted_iota(_I32, bb.shape, 0)
    colb = lax.broadcasted_iota(_I32, (HGRN_DIAG, rows), 1)
    acc = jnp.zeros((HGRN_DIAG, rows), _F32)
    for sl in range(HGRN_DIAG):
        decay = jnp.exp(jnp.where(sub >= sl, bb - bb[sl:sl + 1, :], NEG_BIG))
        sc = jnp.sum(qq * decay * kk[sl:sl + 1, :], axis=-1, keepdims=True)
        acc = jnp.where(colb == r0 + sl, sc, acc)
    d_scr[pl.ds(r0, HGRN_DIAG), :] = acc


def _hgrn_finish(i_ref, g_ref, gain_ref, o_ref, st_ref, b_scr, q_scr, k_scr, d_scr, *, rows):
    b = b_scr[...]
    qf = q_scr[...]
    k = k_scr[...]
    v = i_ref[...]
    row = lax.broadcasted_iota(_I32, (rows, rows), 0)
    col = lax.broadcasted_iota(_I32, (rows, rows), 1)

    scores = d_scr[...]
    rowv = lax.broadcasted_iota(_I32, (rows, V7X_LANES), 0)
    x = rows // 2
    while x >= HGRN_DIAG:
        mids = [jnp.broadcast_to(b[m + x - 1:m + x, :], (2 * x, b.shape[1]))
                for m in range(0, rows, 2 * x)]
        bm = mids[0] if len(mids) == 1 else jnp.concatenate(mids, axis=0)
        upper = (rowv & x) != 0
        w = jnp.exp(-jnp.abs(b - bm))
        qx = jnp.where(upper, qf * w, 0.0).astype(_BF16)
        kx = jnp.where(upper, 0.0, k * w).astype(_BF16)
        sx = lax.dot_general(qx, kx, (((1,), (1,)), ((), ())), preferred_element_type=_F32)
        if 2 * x < rows:
            shift = (2 * x).bit_length() - 1
            sx = jnp.where((row >> shift) == (col >> shift), sx, 0.0)
        scores = scores + sx
        x //= 2

    o = jnp.dot(scores.astype(_BF16), v, preferred_element_type=_F32)

    st = st_ref[...]
    q_in = (qf * jnp.exp(b)).astype(_BF16)
    o = o + lax.dot_general(q_in, st.astype(_BF16), (((1,), (1,)), ((), ())),
                            preferred_element_type=_F32)
    b_last = b[rows - 1:rows, :]
    k_out = (k * jnp.exp(b_last - b)).astype(_BF16)
    upd = lax.dot_general(v, k_out, (((0,), (0,)), ((), ())), preferred_element_type=_F32)
    st_ref[...] = st * jnp.exp(b_last) + upd

    g = g_ref[...].astype(_F32)
    o_ref[...] = (_rms(o, gain_ref[...]) * (g * _sigmoid(g))).astype(o_ref.dtype)


def _hgrn2(p, lb_logits, gain, *, layer, batch, seq, heads, slab_q, slab_f, slab_i, slab_g):
    t = batch * seq
    hd = p.shape[2]
    rows = min(HGRN_ROWS, seq)
    nblk = seq // rows

    hb = HGRN_HEADS
    assert heads % hb == 0 and all(s % hb == 0 for s in (slab_q, slab_f, slab_i, slab_g))

    def slab(first):
        return pl.BlockSpec((hb, rows, hd), lambda b, h, n: (first // hb + h, b * nblk + n, 0))

    return pl.pallas_call(
        functools.partial(_hgrn_body, layer=layer, rows=rows, dk=hd),
        grid=(batch, heads // hb, nblk),
        in_specs=[
            pl.BlockSpec((lb_logits.shape[0], hb * hd), lambda b, h, n: (0, h)),
            slab(slab_q), slab(slab_f), slab(slab_i), slab(slab_g),
            pl.BlockSpec((1, hd), lambda b, h, n: (0, 0)),
        ],
        out_specs=pl.BlockSpec((rows, hb * hd), lambda b, h, n: (b * nblk + n, h)),
        out_shape=jax.ShapeDtypeStruct((t, heads * hd), _BF16),
        scratch_shapes=[
            pltpu.VMEM((hb, hd, hd), _F32),
            pltpu.VMEM((hb, rows, hd), _F32),
            pltpu.VMEM((hb, rows, hd), _F32),
            pltpu.VMEM((hb, rows, hd), _F32),
            pltpu.VMEM((hb, rows, rows), _F32),
        ],
        compiler_params=_params("arbitrary", "arbitrary", "arbitrary"),
        name="hgrn2",
    )(lb_logits, p, p, p, p, gain)


def _fox_decay_body(ff_ref, bias_ref, c_ref, carry_ref):
    @pl.when(pl.program_id(1) == 0)
    def _():
        carry_ref[...] = jnp.zeros_like(carry_ref)

    x = ff_ref[...] + bias_ref[...]
    logsig = jnp.minimum(x, 0.0) - jnp.log(1.0 + jnp.exp(-jnp.abs(x)))
    rows = x.shape[0]
    row = lax.broadcasted_iota(_I32, (rows, rows), 0)
    col = lax.broadcasted_iota(_I32, (rows, rows), 1)
    c = _dot_exact_lhs((row >= col).astype(_BF16), logsig) + carry_ref[...]
    c_ref[...] = c
    carry_ref[...] = c[rows - 1:rows, :]


def _fox_decay(ff, bias_pad, *, batch, seq):
    t = batch * seq
    rows = _pick(seq, (512, 256, 128))
    nblk = seq // rows
    return pl.pallas_call(
        _fox_decay_body,
        grid=(batch, nblk),
        in_specs=[
            pl.BlockSpec((rows, V7X_LANES), lambda b, n: (b * nblk + n, 0)),
            pl.BlockSpec((1, V7X_LANES), lambda b, n: (0, 0)),
        ],
        out_specs=pl.BlockSpec((rows, V7X_LANES), lambda b, n: (b * nblk + n, 0)),
        out_shape=jax.ShapeDtypeStruct((t, V7X_LANES), _F32),
        scratch_shapes=[pltpu.VMEM((1, V7X_LANES), _F32)],
        compiler_params=_params("arbitrary", "arbitrary"),
        name="fox_decay",
    )(ff, bias_pad)


def _fox_body(q_ref, k_ref, v_ref, c_ref, qg_ref, kg_ref, o_ref,
              ka_scr, qa_scr, m_scr, l_scr, acc_scr, s0_scr, s1_scr, p0_scr, p1_scr, *, tq, seq):
    h = pl.program_id(1)
    qi = pl.program_id(2)
    hd = q_ref.shape[2]
    lane = lax.broadcasted_iota(_I32, (tq, V7X_LANES), 1)

    def bias_lanes(c_block, sign, parts_first):
        c = jnp.sum(jnp.where(lane == h, c_block, 0.0), axis=-1, keepdims=True) * (sign * LOG2E)
        parts_at = 0 if parts_first else 3
        ones_at = 3 if parts_first else 0
        x = jnp.where((lane >= ones_at) & (lane < ones_at + 3), 1.0, 0.0)
        for j, part in enumerate(_split3(c)):
            x = jnp.where(lane == parts_at + j, part, x)
        return x.astype(_BF16)

    @pl.when(qi == 0)
    def _():
        def key_block(c, carry):
            r0 = pl.multiple_of(c * tq, tq)
            kk = k_ref[0, pl.ds(r0, tq), :].astype(_F32)
            ka_scr[pl.ds(r0, tq), :hd] = _rms(kk, kg_ref[...]).astype(_BF16)
            ka_scr[pl.ds(r0, tq), hd:] = bias_lanes(c_ref[pl.ds(r0, tq), :], -1.0, True)
            return carry
        lax.fori_loop(0, seq // tq, key_block, 0)

    q0 = pl.multiple_of(qi * tq, tq)
    qn = _rms(q_ref[0].astype(_F32), qg_ref[...]) * ((hd ** -0.5) * LOG2E)
    qa_scr[:, :hd] = qn.astype(_BF16)
    qa_scr[:, hd:] = bias_lanes(c_ref[pl.ds(q0, tq), :], 1.0, False)

    m_scr[...] = jnp.full_like(m_scr, NEG_BIG)
    l_scr[...] = jnp.zeros_like(l_scr)
    acc_scr[...] = jnp.zeros_like(acc_scr)

    s_bufs = (s0_scr, s1_scr)
    p_bufs = (p0_scr, p1_scr)

    def scores(j):
        r0 = pl.multiple_of(j * tq, tq)
        return lax.dot_general(qa_scr[...], ka_scr[pl.ds(r0, tq), :], (((1,), (1,)), ((), ())),
                               preferred_element_type=_F32)

    def softmax(s):
        m_old = m_scr[...]
        m_new = jnp.maximum(m_old, jnp.max(s, axis=-1, keepdims=True))
        alpha = jnp.exp2(m_old - m_new)
        p = jnp.exp2(s - jnp.tile(m_new, (1, tq // V7X_LANES)))
        l_scr[...] = alpha * l_scr[...] + jnp.sum(p, axis=-1, keepdims=True)
        m_scr[...] = m_new
        return alpha, p.astype(_BF16)

    def weighted_values(p, j):
        r0 = pl.multiple_of(j * tq, tq)
        return jnp.dot(p, v_ref[0, pl.ds(r0, tq), :], preferred_element_type=_F32)

    def full_block(j, buf):
        other = 1 - buf
        s_bufs[other][...] = scores(j + 1)
        alpha, p = softmax(s_bufs[buf][...])
        done = weighted_values(p_bufs[other][...], jnp.maximum(j - 1, 0))
        p_bufs[buf][...] = p
        acc_scr[...] = alpha * (acc_scr[...] + done)

    def diagonal_block(buf):
        row = lax.broadcasted_iota(_I32, (tq, tq), 0)
        col = lax.broadcasted_iota(_I32, (tq, tq), 1)
        alpha, p = softmax(jnp.where(col <= row, s_bufs[buf][...], NEG_BIG))
        done = weighted_values(p_bufs[1 - buf][...], jnp.maximum(qi - 1, 0))
        acc = alpha * (acc_scr[...] + done) + weighted_values(p, qi)
        o_ref[...] = (acc / l_scr[...]).astype(o_ref.dtype)

    p1_scr[...] = jnp.zeros_like(p1_scr)
    s0_scr[...] = scores(0)

    def block_pair(jj, carry):
        full_block(2 * jj, 0)
        full_block(2 * jj + 1, 1)
        return carry

    lax.fori_loop(0, qi // 2, block_pair, 0)

    @pl.when((qi & 1) == 0)
    def _():
        diagonal_block(0)

    @pl.when((qi & 1) == 1)
    def _():
        full_block(qi - 1, 0)
        diagonal_block(1)


def _fox_attn(p, c, q_gain, k_gain, *, batch, seq, heads, slab_q, slab_k, slab_v):
    t = batch * seq
    hd = p.shape[2]
    tq = _pick(seq, (512, 256))
    nq = seq // tq
    return pl.pallas_call(
        functools.partial(_fox_body, tq=tq, seq=seq),
        grid=(batch, heads, nq),
        in_specs=[
            pl.BlockSpec((1, tq, hd), lambda b, h, i: (slab_q + h, b * nq + i, 0)),
            pl.BlockSpec((1, seq, hd), lambda b, h, i: (slab_k + h, b, 0)),
            pl.BlockSpec((1, seq, hd), lambda b, h, i: (slab_v + h, b, 0)),
            pl.BlockSpec((seq, V7X_LANES), lambda b, h, i: (b, 0)),
            pl.BlockSpec((1, hd), lambda b, h, i: (0, 0)),
            pl.BlockSpec((1, hd), lambda b, h, i: (0, 0)),
        ],
        out_specs=pl.BlockSpec((tq, hd), lambda b, h, i: (b * nq + i, h)),
        out_shape=jax.ShapeDtypeStruct((t, heads * hd), _BF16),
        scratch_shapes=[
            pltpu.VMEM((seq, 2 * hd), _BF16),
            pltpu.VMEM((tq, 2 * hd), _BF16),
            pltpu.VMEM((tq, V7X_LANES), _F32),
            pltpu.VMEM((tq, V7X_LANES), _F32),
            pltpu.VMEM((tq, hd), _F32),
            pltpu.VMEM((tq, tq), _F32),
            pltpu.VMEM((tq, tq), _F32),
            pltpu.VMEM((tq, tq), _BF16),
            pltpu.VMEM((tq, tq), _BF16),
        ],
        compiler_params=_params("arbitrary", "arbitrary", "arbitrary"),
        name="fox_attn",
    )(p, p, p, c, q_gain, k_gain)


def _merge_body(a_ref, b_ref, ga_ref, gb_ref, x_ref, wa_ref, wb_ref, wo_ref, n2_ref, rw_ref,
                rb_ref, x2_ref, h2_ref, route_ref, gate_ref, cnt_ref, carry_ref, *, rpt):
    @pl.when(pl.program_id(0) == 0)
    def _():
        carry_ref[...] = jnp.zeros_like(carry_ref)

    def gates(ref):
        parts = [ref[s] for s in range(ref.shape[0])]
        g = parts[0] if len(parts) == 1 else jnp.concatenate(parts, axis=-1)
        return _sigmoid(g.astype(_F32))

    ua = jnp.dot(a_ref[...], wa_ref[...], preferred_element_type=_F32)
    ub = jnp.dot(b_ref[...], wb_ref[...], preferred_element_type=_F32)
    merged = gates(ga_ref) * ua + gates(gb_ref) * ub
    x2 = x_ref[...] + jnp.dot(merged.astype(_BF16), wo_ref[...], preferred_element_type=_F32)
    x2_ref[...] = x2
    h2 = _rms(x2, n2_ref[...])
    _store_token_tiles(h2_ref, h2, rpt)

    logits = _dot_f32(h2, rw_ref[...]) + rb_ref[...]
    tm = logits.shape[0]
    lane = lax.broadcasted_iota(_I32, logits.shape, 1)
    lane_f = lane.astype(_F32)
    work = logits
    top_v, top_i = [], []
    for _ in range(TOP_K):
        mx = jnp.max(work, axis=-1, keepdims=True)
        ix = jnp.min(jnp.where(work == mx, lane_f, float(V7X_LANES)), axis=-1,
                     keepdims=True).astype(_I32)
        top_v.append(mx)
        top_i.append(ix)
        work = jnp.where(lane == ix, NEG_BIG, work)

    ex = [jnp.exp(tv - top_v[0]) for tv in top_v]
    denom = ex[0]
    for t_ in ex[1:]:
        denom = denom + t_

    onehot = jnp.zeros(logits.shape, _F32)
    for ix in top_i:
        onehot = onehot + (lane == ix).astype(_F32)
    row = lax.broadcasted_iota(_I32, (tm, tm), 0)
    col = lax.broadcasted_iota(_I32, (tm, tm), 1)
    before = jnp.dot((row > col).astype(_BF16), onehot.astype(_BF16),
                     preferred_element_type=_F32) + carry_ref[...]

    route = jnp.zeros(logits.shape, _I32)
    gate = jnp.zeros(logits.shape, _F32)
    for r in range(TOP_K):
        rank = jnp.sum(jnp.where(lane == top_i[r], before, 0.0), axis=-1, keepdims=True)
        route = jnp.where(lane == r, top_i[r], route)
        route = jnp.where(lane == TOP_K + r, rank.astype(_I32), route)
        gate = jnp.where(lane == r, ex[r] / denom, gate)
    route_ref[...] = route
    gate_ref[...] = gate

    carry = carry_ref[...] + jnp.sum(onehot, axis=0, keepdims=True)
    carry_ref[...] = carry
    cnt_ref[...] = carry.astype(_I32)


def _merge(a, b, p, x2d, wa, wb, wo, n2, rw_pad, rb_pad, *, slab_ga, slab_gb, rpt):
    t, d = x2d.shape
    tm = _pick(t, (256, 128))
    gs = d // V7X_LANES
    const = lambda i: (0, 0)
    once = pl.Buffered(1)
    return pl.pallas_call(
        functools.partial(_merge_body, rpt=rpt),
        grid=(t // tm,),
        in_specs=[
            pl.BlockSpec((tm, a.shape[1]), lambda i: (i, 0)),
            pl.BlockSpec((tm, b.shape[1]), lambda i: (i, 0)),
            pl.BlockSpec((gs, tm, V7X_LANES), lambda i: (slab_ga // gs, i, 0)),
            pl.BlockSpec((gs, tm, V7X_LANES), lambda i: (slab_gb // gs, i, 0)),
            pl.BlockSpec((tm, d), lambda i: (i, 0)),
            pl.BlockSpec(wa.shape, const, pipeline_mode=once),
            pl.BlockSpec(wb.shape, const, pipeline_mode=once),
            pl.BlockSpec(wo.shape, const, pipeline_mode=once),
            pl.BlockSpec((1, d), const),
            pl.BlockSpec(rw_pad.shape, const, pipeline_mode=once),
            pl.BlockSpec((1, V7X_LANES), const),
        ],
        out_specs=[
            pl.BlockSpec((tm, d), lambda i: (i, 0)),
            pl.BlockSpec((tm * rpt, V7X_LANES), lambda i: (i, 0)),
            pl.BlockSpec((tm, V7X_LANES), lambda i: (i, 0)),
            pl.BlockSpec((tm, V7X_LANES), lambda i: (i, 0)),
            pl.BlockSpec((1, V7X_LANES), const),
        ],
        out_shape=[
            jax.ShapeDtypeStruct((t, d), _F32),
            jax.ShapeDtypeStruct((t * rpt, V7X_LANES), _U32),
            jax.ShapeDtypeStruct((t, V7X_LANES), _I32),
            jax.ShapeDtypeStruct((t, V7X_LANES), _F32),
            jax.ShapeDtypeStruct((1, V7X_LANES), _I32),
        ],
        scratch_shapes=[pltpu.VMEM((1, V7X_LANES), _F32)],
        compiler_params=_params("arbitrary"),
        name="merge_router",
    )(a, b, p, p, x2d, wa, wb, wo, n2, rw_pad, rb_pad)


def _start_and_wait_all(n, make_copy):
    assert n % DMA_UNROLL == 0

    def start(i, carry):
        for u in range(DMA_UNROLL):
            make_copy(i * DMA_UNROLL + u).start(priority=u % 2)
        return carry

    def wait(i, carry):
        for u in range(DMA_UNROLL):
            make_copy(i * DMA_UNROLL + u).wait()
        return carry

    lax.fori_loop(0, n // DMA_UNROLL, start, 0)
    lax.fori_loop(0, n // DMA_UNROLL, wait, 0)


def _tile_copy(src, src_tile, dst, dst_tile, sem, rpt, tiles=1):
    def first_row(tile):
        return tile * rpt if isinstance(tile, int) else pl.multiple_of(tile * rpt, rpt)

    n = tiles * rpt
    return pltpu.make_async_copy(src.at[pl.ds(first_row(src_tile), n)],
                                 dst.at[pl.ds(first_row(dst_tile), n)], sem)


def _dispatch_body(pstart_ref, cnt_ref, eidx_ref, rank_ref, h2_ref, xs_ref, zeros_scr,
                   sem, zsem, *, tokens, n_exp, n_rows, rpt):
    step = pl.program_id(0)
    n_assign = tokens * TOP_K

    def token_copy(a):
        dst = pstart_ref[eidx_ref[a]] + rank_ref[a]
        return _tile_copy(h2_ref, a >> TOP_K_SHIFT, xs_ref, dst, sem, rpt)


    def fill(do):
        def expert_pad(e, carry):
            first = pstart_ref[e] + cnt_ref[e]

            def pad_tile(c, carry):
                do(_tile_copy(zeros_scr, 0, xs_ref, first + c, zsem, rpt))
                return carry

            lax.fori_loop(0, pstart_ref[e + 1] - first, pad_tile, 0)
            return carry

        lax.fori_loop(0, n_exp, expert_pad, 0)

        tail = pstart_ref[n_exp]

        def tail_block(c, carry):
            do(_tile_copy(zeros_scr, 0, xs_ref, tail + c * EXPERT_ROWS, zsem, rpt, EXPERT_ROWS))
            return carry

        lax.fori_loop(0, (n_rows - tail) // EXPERT_ROWS, tail_block, 0)

    @pl.when(step == 0)
    def _():
        zeros_scr[...] = jnp.zeros_like(zeros_scr)
        fill(lambda c: c.start())

    _start_and_wait_all(n_assign, token_copy)

    @pl.when(step == 0)
    def _():
        fill(lambda c: c.wait())


def _dispatch(h2, eidx, rank, pstart, counts, *, n_rows, rpt):
    t = h2.shape[0] // rpt
    tokens = _pick(t, (256, 128))
    n_exp = counts.shape[0]
    smem = lambda: pl.BlockSpec((tokens * TOP_K,), lambda i, *_: (i,), memory_space=pltpu.SMEM)
    return pl.pallas_call(
        functools.partial(_dispatch_body, tokens=tokens, n_exp=n_exp, n_rows=n_rows, rpt=rpt),
        grid_spec=pltpu.PrefetchScalarGridSpec(
            num_scalar_prefetch=2,
            grid=(t // tokens,),
            in_specs=[smem(), smem(),
                      pl.BlockSpec((tokens * rpt, V7X_LANES), lambda i, *_: (i, 0))],
            out_specs=pl.BlockSpec(memory_space=pl.ANY),
            scratch_shapes=[pltpu.VMEM((EXPERT_ROWS * rpt, V7X_LANES), _U32),
                            pltpu.SemaphoreType.DMA, pltpu.SemaphoreType.DMA],
        ),
        out_shape=jax.ShapeDtypeStruct((n_rows * rpt, V7X_LANES), _U32),
        compiler_params=_params("arbitrary"),
        name="dispatch",
    )(pstart, counts, eidx, rank, h2)


def _expert_block_state(bexp_ref, nvalid_ref):
    i = pl.program_id(1)
    valid = i < nvalid_ref[0]
    changed = jnp.logical_or(i == 0, bexp_ref[i] != bexp_ref[jnp.maximum(i - 1, 0)])
    return valid, jnp.logical_and(valid, changed)


def _expert_gu_body(bexp_ref, nvalid_ref, x_ref, wg_ref, wu_ref, bg_ref, bu_ref, o_ref, w_scr,
                    *, rpt):
    valid, changed = _expert_block_state(bexp_ref, nvalid_ref)

    @pl.when(changed)
    def _():
        w_scr[0] = wg_ref[0].astype(_BF16)
        w_scr[1] = wu_ref[0].astype(_BF16)

    @pl.when(valid)
    def _():
        slabs = _load_token_tiles(x_ref, o_ref.shape[0], rpt)
        x = jnp.concatenate([s.astype(_BF16) for s in slabs], axis=-1)
        gate = jnp.dot(x, w_scr[0], preferred_element_type=_F32) + bg_ref[0]
        up = jnp.dot(x, w_scr[1], preferred_element_type=_F32) + bu_ref[0]
        gate = jnp.minimum(gate, SWIGLU_LIMIT)
        up = jnp.clip(up, -SWIGLU_LIMIT, SWIGLU_LIMIT)
        o_ref[...] = ((up + 1.0) * gate * _sigmoid(SWIGLU_ALPHA * gate)).astype(o_ref.dtype)

    @pl.when(jnp.logical_not(valid))
    def _():
        o_ref[...] = jnp.zeros_like(o_ref)


def _block_index(i, nvalid_ref):
    return jnp.minimum(i, nvalid_ref[0] - 1)


def _expert_gu(xs, w_gu, b_gu, bexp, nvalid, *, rpt):
    n_rows = xs.shape[0] // rpt
    n_exp, d, two_de = w_gu.shape
    de = two_de // 2
    tm = EXPERT_ROWS
    tn = _pick(de, (512, 256, 128))
    nj = de // tn
    blk = _block_index
    return pl.pallas_call(
        functools.partial(_expert_gu_body, rpt=rpt),
        grid_spec=pltpu.PrefetchScalarGridSpec(
            num_scalar_prefetch=2,
            grid=(nj, n_rows // tm),
            in_specs=[
                pl.BlockSpec((tm * rpt, V7X_LANES), lambda j, i, be, nv: (blk(i, nv), 0)),
                pl.BlockSpec((1, d, tn), lambda j, i, be, nv: (be[blk(i, nv)], 0, j)),
                pl.BlockSpec((1, d, tn), lambda j, i, be, nv: (be[blk(i, nv)], 0, nj + j)),
                pl.BlockSpec((1, 1, tn), lambda j, i, be, nv: (be[blk(i, nv)], 0, j)),
                pl.BlockSpec((1, 1, tn), lambda j, i, be, nv: (be[blk(i, nv)], 0, nj + j)),
            ],
            out_specs=pl.BlockSpec((tm, tn), lambda j, i, be, nv: (i, j)),
            scratch_shapes=[pltpu.VMEM((2, d, tn), _BF16)],
        ),
        out_shape=jax.ShapeDtypeStruct((n_rows, de), _BF16),
        compiler_params=_params("arbitrary", "arbitrary"),
        name="expert_gu",
    )(bexp, nvalid, xs, w_gu, w_gu, b_gu.reshape(n_exp, 1, two_de), b_gu.reshape(n_exp, 1, two_de))


def _expert_down_body(bexp_ref, nvalid_ref, a_ref, w_ref, b_ref, o_ref, w_scr, *, rpt):
    valid, changed = _expert_block_state(bexp_ref, nvalid_ref)

    @pl.when(changed)
    def _():
        w_scr[...] = w_ref[0].astype(_BF16)

    @pl.when(valid)
    def _():
        out = jnp.dot(a_ref[...], w_scr[...], preferred_element_type=_F32) + b_ref[0]
        _store_token_tiles(o_ref, out, rpt)

    @pl.when(jnp.logical_not(valid))
    def _():
        o_ref[...] = jnp.zeros_like(o_ref)


def _expert_down(act, w_down, b_down, bexp, nvalid, *, rpt):
    n_rows, de = act.shape
    n_exp, _, d = w_down.shape
    tm = EXPERT_ROWS
    blk = _block_index
    return pl.pallas_call(
        functools.partial(_expert_down_body, rpt=rpt),
        grid_spec=pltpu.PrefetchScalarGridSpec(
            num_scalar_prefetch=2,
            grid=(1, n_rows // tm),
            in_specs=[
                pl.BlockSpec((tm, de), lambda j, i, be, nv: (blk(i, nv), 0)),
                pl.BlockSpec((1, de, d), lambda j, i, be, nv: (be[blk(i, nv)], 0, 0),
                             pipeline_mode=pl.Buffered(1)),
                pl.BlockSpec((1, 1, d), lambda j, i, be, nv: (be[blk(i, nv)], 0, 0)),
            ],
            out_specs=pl.BlockSpec((tm * rpt, V7X_LANES), lambda j, i, be, nv: (i, 0)),
            scratch_shapes=[pltpu.VMEM((de, d), _BF16)],
        ),
        out_shape=jax.ShapeDtypeStruct((n_rows * rpt, V7X_LANES), _U32),
        compiler_params=_params("arbitrary", "arbitrary"),
        name="expert_down",
    )(bexp, nvalid, act, w_down, b_down.reshape(n_exp, 1, d))


def _combine_body(pstart_ref, eidx_ref, rank_ref, gate_ref, x2_ref, rows_ref, o_ref, buf, sem,
                  *, tokens, rpt):
    n_assign = tokens * TOP_K

    def gather(a):
        src = pstart_ref[eidx_ref[a]] + rank_ref[a]
        slot = (a & (TOP_K - 1)) * tokens + (a >> TOP_K_SHIFT)
        return _tile_copy(rows_ref, src, buf, slot, sem, rpt)

    _start_and_wait_all(n_assign, gather)

    gate = gate_ref[...]
    acc = [None] * (2 * rpt)
    for r in range(TOP_K):
        g = gate[:, r:r + 1]
        for s, slab in enumerate(_load_token_tiles(buf, tokens, rpt, first=r * tokens * rpt)):
            acc[s] = g * slab if acc[s] is None else acc[s] + g * slab
    o_ref[...] = x2_ref[...] + jnp.concatenate(acc, axis=-1)


def _combine(x2, rows, eidx, rank, gate, pstart, *, rpt):
    t, d = x2.shape
    tokens = _pick(t, (256, 128))
    smem = lambda: pl.BlockSpec((tokens * TOP_K,), lambda i, *_: (i,), memory_space=pltpu.SMEM)
    return pl.pallas_call(
        functools.partial(_combine_body, tokens=tokens, rpt=rpt),
        grid_spec=pltpu.PrefetchScalarGridSpec(
            num_scalar_prefetch=1,
            grid=(t // tokens,),
            in_specs=[smem(), smem(),
                      pl.BlockSpec((tokens, V7X_LANES), lambda i, *_: (i, 0)),
                      pl.BlockSpec((tokens, d), lambda i, *_: (i, 0)),
                      pl.BlockSpec(memory_space=pl.ANY)],
            out_specs=pl.BlockSpec((tokens, d), lambda i, *_: (i, 0)),
            scratch_shapes=[pltpu.VMEM((TOP_K * tokens * rpt, V7X_LANES), _U32),
                            pltpu.SemaphoreType.DMA],
        ),
        out_shape=jax.ShapeDtypeStruct((t, d), _F32),
        compiler_params=_params("arbitrary"),
        name="combine",
    )(pstart, eidx, rank, gate, x2, rows)


def _pad_lanes(v, fill=0.0):
    v = v.reshape(1, -1).astype(_F32)
    return jnp.pad(v, ((0, 0), (0, V7X_LANES - v.shape[1])), constant_values=fill)


def kernel(x, lb_logits, norm1, w_in, hg_norm, fox_q_norm, fox_k_norm, fox_f_bias, w_up_hg, w_up_fox,
           w_out, norm2, router_w, router_b, w_gu, b_gu, w_down, b_down):
    batch, seq, d = x.shape
    t = batch * seq
    depth = w_in.shape[0]
    hd = hg_norm.shape[1]
    hg_heads = w_up_hg.shape[1] // hd
    fox_heads = fox_f_bias.shape[1]
    hg_w, fox_w = hg_heads * hd, fox_heads * hd
    n_exp = router_w.shape[2]
    assert hd == V7X_LANES and fox_q_norm.shape[1] == hd and lb_logits.shape[1] == hg_w
    assert n_exp <= V7X_LANES and d % (2 * V7X_LANES) == 0
    assert w_in.shape[2] == 4 * hg_w + 3 * fox_w + fox_heads + 2 * d
    rpt = d // (2 * V7X_LANES)

    ff0 = 4 * hg_w + 3 * fox_w
    slab_ga = 0
    slab_gb = d // V7X_LANES
    slab_hq = 2 * (d // V7X_LANES)
    slab_fq = slab_hq + 4 * hg_heads

    n_rows = t * TOP_K + n_exp * EXPERT_ROWS
    xf = x.reshape(t, d)
    for l in range(depth):
        w = w_in[l]
        w_main = jnp.concatenate([w[:, ff0 + fox_heads:], w[:, :ff0]], axis=1).astype(_BF16)
        w_ff = jnp.pad(w[:, ff0:ff0 + fox_heads], ((0, 0), (0, V7X_LANES - fox_heads)))
        p, ff = _in_proj(xf, norm1[l].reshape(1, d), w_main, w_ff)

        a = _hgrn2(p, lb_logits, hg_norm[l].reshape(1, hd), layer=l, batch=batch, seq=seq,
                   heads=hg_heads, slab_q=slab_hq, slab_f=slab_hq + hg_heads,
                   slab_i=slab_hq + 2 * hg_heads, slab_g=slab_hq + 3 * hg_heads)

        c = _fox_decay(ff, _pad_lanes(fox_f_bias[l]), batch=batch, seq=seq)
        b = _fox_attn(p, c, fox_q_norm[l].reshape(1, hd), fox_k_norm[l].reshape(1, hd),
                      batch=batch, seq=seq, heads=fox_heads, slab_q=slab_fq,
                      slab_k=slab_fq + fox_heads, slab_v=slab_fq + 2 * fox_heads)

        rw_pad = jnp.pad(router_w[l], ((0, 0), (0, V7X_LANES - n_exp)))
        x2, h2, route, gate, counts = _merge(
            a, b, p, xf, w_up_hg[l].astype(_BF16), w_up_fox[l].astype(_BF16),
            w_out[l].astype(_BF16), norm2[l].reshape(1, d), rw_pad,
            _pad_lanes(router_b[l], NEG_BIG), slab_ga=slab_ga, slab_gb=slab_gb, rpt=rpt)

        counts = counts[0, :n_exp]
        padded = (counts + EXPERT_ROWS - 1) // EXPERT_ROWS * EXPERT_ROWS
        upto = jnp.arange(n_exp)[:, None] >= jnp.arange(n_exp)[None, :]
        pend = jnp.sum(jnp.where(upto, padded[None, :], 0), axis=1).astype(_I32)
        pstart = jnp.concatenate([jnp.zeros((1,), _I32), pend])
        n_blocks = n_rows // EXPERT_ROWS
        block_first = jnp.arange(n_blocks, dtype=_I32) * EXPERT_ROWS
        bexp = jnp.minimum(jnp.sum(pend[None, :] <= block_first[:, None], axis=1),
                           n_exp - 1).astype(_I32)
        nvalid = (pend[-1:] // EXPERT_ROWS).astype(_I32)
        eidx = route[:, :TOP_K].reshape(-1)
        rank = route[:, TOP_K:2 * TOP_K].reshape(-1)

        xs = _dispatch(h2, eidx, rank, pstart, counts, n_rows=n_rows, rpt=rpt)
        act = _expert_gu(xs, w_gu[l], b_gu[l], bexp, nvalid, rpt=rpt)
        rows = _expert_down(act, w_down[l], b_down[l], bexp, nvalid, rpt=rpt)
        xf = _combine(x2, rows, eidx, rank, gate, pstart, rpt=rpt)
    return xf.reshape(batch, seq, d)
```

```python
import functools

import jax
import jax.numpy as jnp
from jax import lax
from jax.experimental import pallas as pl
from jax.experimental.pallas import tpu as pltpu

_F32 = jnp.float32
_BF16 = jnp.bfloat16
_I32 = jnp.int32
_U32 = jnp.uint32

V7X_LANES = 128
V7X_SUBLANES = 8
V7X_VMEM_BYTES = 64 * 1024 * 1024
VMEM_LIMIT_BYTES = V7X_VMEM_BYTES * 7 // 8

EPS = 1e-6
SWIGLU_LIMIT = 7.0
SWIGLU_ALPHA = 1.702
TOP_K = 4
TOP_K_SHIFT = TOP_K.bit_length() - 1
assert 1 << TOP_K_SHIFT == TOP_K
NEG_BIG = -1e30
LOG2E = 1.4426950408889634

HGRN_ROWS = 256
HGRN_DIAG = V7X_SUBLANES
HGRN_HEADS = 8
EXPERT_ROWS = 512
DMA_UNROLL = 8


def _params(*semantics):
    return pltpu.CompilerParams(dimension_semantics=semantics, vmem_limit_bytes=VMEM_LIMIT_BYTES)


def _pick(n, candidates):
    for c in candidates:
        if n % c == 0:
            return c
    raise ValueError(f"no tile in {candidates} divides {n}")


def _sigmoid(x):
    return 1.0 / (1.0 + jnp.exp(-x))


def _split3(x):
    hi = x.astype(_BF16).astype(_F32)
    r = x - hi
    mid = r.astype(_BF16).astype(_F32)
    lo = (r - mid).astype(_BF16).astype(_F32)
    return hi, mid, lo


def _dot_exact_lhs(lhs_bf16, x):
    out = None
    for part in _split3(x):
        t = jnp.dot(lhs_bf16, part.astype(_BF16), preferred_element_type=_F32)
        out = t if out is None else out + t
    return out


def _dot_f32(a, b):
    a_hi, a_lo, _ = _split3(a)
    b_hi, b_lo, _ = _split3(b)
    a_hi, a_lo, b_hi, b_lo = (t.astype(_BF16) for t in (a_hi, a_lo, b_hi, b_lo))
    return (jnp.dot(a_hi, b_hi, preferred_element_type=_F32)
            + jnp.dot(a_hi, b_lo, preferred_element_type=_F32)
            + jnp.dot(a_lo, b_hi, preferred_element_type=_F32))


def _rms(x, gain):
    return x * lax.rsqrt(jnp.mean(x * x, axis=-1, keepdims=True) + EPS) * gain


def _pack_pair(lo, hi):
    lo_bits = lax.bitcast_convert_type(lo.astype(_BF16).astype(_F32), _U32)
    hi_bits = lax.bitcast_convert_type(hi.astype(_BF16).astype(_F32), _U32)
    return (lo_bits >> 16) | (hi_bits & jnp.uint32(0xFFFF0000))


def _unpack_pair(words):
    lo = lax.bitcast_convert_type(words << 16, _F32)
    hi = lax.bitcast_convert_type(words & jnp.uint32(0xFFFF0000), _F32)
    return lo, hi


def _store_token_tiles(ref, x, rpt):
    rows = x.shape[0]
    for k in range(rpt):
        lo = x[:, (2 * k) * V7X_LANES:(2 * k + 1) * V7X_LANES]
        hi = x[:, (2 * k + 1) * V7X_LANES:(2 * k + 2) * V7X_LANES]
        ref[pl.ds(k, rows, stride=rpt), :] = _pack_pair(lo, hi)


def _load_token_tiles(ref, rows, rpt, first=0):
    slabs = []
    for k in range(rpt):
        slabs.extend(_unpack_pair(ref[pl.ds(first + k, rows, stride=rpt), :]))
    return slabs


def _inproj_body(x_ref, g_ref, w_ref, wff_ref, p_ref, ff_ref, h_scr):
    @pl.when(pl.program_id(1) == 0)
    def _():
        h = _rms(x_ref[...], g_ref[...])
        h_scr[...] = h.astype(_BF16)
        ff_ref[...] = _dot_f32(h, wff_ref[...])

    acc = jnp.dot(h_scr[...], w_ref[...], preferred_element_type=_F32)
    for s in range(p_ref.shape[0]):
        p_ref[s] = acc[:, s * V7X_LANES:(s + 1) * V7X_LANES].astype(_BF16)


def _in_proj(x2d, gain, w_main, w_ff):
    t, d = x2d.shape
    n = w_main.shape[1]
    tm = _pick(t, (1024, 512, 256, 128))
    tn = _pick(n, (1024, 512, 256, 128))
    slabs = tn // V7X_LANES
    return pl.pallas_call(
        _inproj_body,
        grid=(t // tm, n // tn),
        in_specs=[
            pl.BlockSpec((tm, d), lambda i, j: (i, 0)),
            pl.BlockSpec((1, d), lambda i, j: (0, 0)),
            pl.BlockSpec((d, tn), lambda i, j: (0, j)),
            pl.BlockSpec((d, V7X_LANES), lambda i, j: (0, 0)),
        ],
        out_specs=[
            pl.BlockSpec((slabs, tm, V7X_LANES), lambda i, j: (j, i, 0)),
            pl.BlockSpec((tm, V7X_LANES), lambda i, j: (i, 0)),
        ],
        out_shape=[
            jax.ShapeDtypeStruct((n // V7X_LANES, t, V7X_LANES), _BF16),
            jax.ShapeDtypeStruct((t, V7X_LANES), _F32),
        ],
        scratch_shapes=[pltpu.VMEM((tm, d), _BF16)],
        compiler_params=_params("arbitrary", "arbitrary"),
        name="in_proj",
    )(x2d, gain, w_main, w_ff)


def _hgrn_body(lbl_ref, q_ref, f_ref, i_ref, g_ref, gain_ref, o_ref,
               st_ref, b_scr, q_scr, k_scr, d_scr, *, layer, rows, dk):
    @pl.when(pl.program_id(2) == 0)
    def _():
        st_ref[...] = jnp.zeros_like(st_ref)

    heads = range(q_ref.shape[0])
    for hb in heads:
        _hgrn_prepare(lbl_ref.at[:, pl.ds(hb * dk, dk)], q_ref.at[hb], f_ref.at[hb],
                      b_scr.at[hb], q_scr.at[hb], k_scr.at[hb], layer=layer, rows=rows, dk=dk)

    def diag_blocks(gi, carry):
        for hb in heads:
            _hgrn_diag_block(gi, b_scr.at[hb], q_scr.at[hb], k_scr.at[hb], d_scr.at[hb], rows)
        return carry

    lax.fori_loop(0, rows // HGRN_DIAG, diag_blocks, 0, unroll=4)

    for hb in heads:
        _hgrn_finish(i_ref.at[hb], g_ref.at[hb], gain_ref, o_ref.at[:, pl.ds(hb * dk, dk)],
                     st_ref.at[hb], b_scr.at[hb], q_scr.at[hb], k_scr.at[hb], d_scr.at[hb],
                     rows=rows)


def _hgrn_prepare(lbl_ref, q_ref, f_ref, b_scr, q_scr, k_scr, *, layer, rows, dk):
    lbl = lbl_ref[...]
    e = jnp.exp(lbl - jnp.max(lbl, axis=0, keepdims=True))
    lb = jnp.sum(e[:layer + 1], axis=0, keepdims=True) / jnp.sum(e, axis=0, keepdims=True)

    q = q_ref[...].astype(_F32)
    qf = q * _sigmoid(q) * (dk ** -0.5)
    fg = lb + (1.0 - lb) * _sigmoid(f_ref[...].astype(_F32))
    row = lax.broadcasted_iota(_I32, (rows, rows), 0)
    col = lax.broadcasted_iota(_I32, (rows, rows), 1)
    b_scr[...] = _dot_exact_lhs((row >= col).astype(_BF16), jnp.log(fg))
    q_scr[...] = qf
    k_scr[...] = 1.0 - fg


def _hgrn_diag_block(gi, b_scr, q_scr, k_scr, d_scr, rows):
    r0 = pl.multiple_of(gi * HGRN_DIAG, HGRN_DIAG)
    bb = b_scr[pl.ds(r0, HGRN_DIAG), :]
    qq = q_scr[pl.ds(r0, HGRN_DIAG), :]
    kk = k_scr[pl.ds(r0, HGRN_DIAG), :]
    sub = lax.broadcasted_iota(_I32, bb.shape, 0)
    colb = lax.broadcasted_iota(_I32, (HGRN_DIAG, rows), 1)
    acc = jnp.zeros((HGRN_DIAG, rows), _F32)
    for sl in range(HGRN_DIAG):
        decay = jnp.exp(jnp.where(sub >= sl, bb - bb[sl:sl + 1, :], NEG_BIG))
        sc = jnp.sum(qq * decay * kk[sl:sl + 1, :], axis=-1, keepdims=True)
        acc = jnp.where(colb == r0 + sl, sc, acc)
    d_scr[pl.ds(r0, HGRN_DIAG), :] = acc


def _hgrn_finish(i_ref, g_ref, gain_ref, o_ref, st_ref, b_scr, q_scr, k_scr, d_scr, *, rows):
    b = b_scr[...]
    qf = q_scr[...]
    k = k_scr[...]
    v = i_ref[...]
    row = lax.broadcasted_iota(_I32, (rows, rows), 0)
    col = lax.broadcasted_iota(_I32, (rows, rows), 1)

    scores = d_scr[...]
    rowv = lax.broadcasted_iota(_I32, (rows, V7X_LANES), 0)
    x = rows // 2
    while x >= HGRN_DIAG:
        mids = [jnp.broadcast_to(b[m + x - 1:m + x, :], (2 * x, b.shape[1]))
                for m in range(0, rows, 2 * x)]
        bm = mids[0] if len(mids) == 1 else jnp.concatenate(mids, axis=0)
        upper = (rowv & x) != 0
        w = jnp.exp(-jnp.abs(b - bm))
        qx = jnp.where(upper, qf * w, 0.0).astype(_BF16)
        kx = jnp.where(upper, 0.0, k * w).astype(_BF16)
        sx = lax.dot_general(qx, kx, (((1,), (1,)), ((), ())), preferred_element_type=_F32)
        if 2 * x < rows:
            shift = (2 * x).bit_length() - 1
            sx = jnp.where((row >> shift) == (col >> shift), sx, 0.0)
        scores = scores + sx
        x //= 2

    o = jnp.dot(scores.astype(_BF16), v, preferred_element_type=_F32)

    st = st_ref[...]
    q_in = (qf * jnp.exp(b)).astype(_BF16)
    o = o + lax.dot_general(q_in, st.astype(_BF16), (((1,), (1,)), ((), ())),
                            preferred_element_type=_F32)
    b_last = b[rows - 1:rows, :]
    k_out = (k * jnp.exp(b_last - b)).astype(_BF16)
    upd = lax.dot_general(v, k_out, (((0,), (0,)), ((), ())), preferred_element_type=_F32)
    st_ref[...] = st * jnp.exp(b_last) + upd

    g = g_ref[...].astype(_F32)
    o_ref[...] = (_rms(o, gain_ref[...]) * (g * _sigmoid(g))).astype(o_ref.dtype)


def _hgrn2(p, lb_logits, gain, *, layer, batch, seq, heads, slab_q, slab_f, slab_i, slab_g):
    t = batch * seq
    hd = p.shape[2]
    rows = min(HGRN_ROWS, seq)
    nblk = seq // rows

    hb = next(c for c in range(HGRN_HEADS, 0, -1)
              if heads % c == 0 and all(s % c == 0 for s in (slab_q, slab_f, slab_i, slab_g)))

    def slab(first):
        return pl.BlockSpec((hb, rows, hd), lambda b, h, n: (first // hb + h, b * nblk + n, 0))

    return pl.pallas_call(
        functools.partial(_hgrn_body, layer=layer, rows=rows, dk=hd),
        grid=(batch, heads // hb, nblk),
        in_specs=[
            pl.BlockSpec((lb_logits.shape[0], hb * hd), lambda b, h, n: (0, h)),
            slab(slab_q), slab(slab_f), slab(slab_i), slab(slab_g),
            pl.BlockSpec((1, hd), lambda b, h, n: (0, 0)),
        ],
        out_specs=pl.BlockSpec((rows, hb * hd), lambda b, h, n: (b * nblk + n, h)),
        out_shape=jax.ShapeDtypeStruct((t, heads * hd), _BF16),
        scratch_shapes=[
            pltpu.VMEM((hb, hd, hd), _F32),
            pltpu.VMEM((hb, rows, hd), _F32),
            pltpu.VMEM((hb, rows, hd), _F32),
            pltpu.VMEM((hb, rows, hd), _F32),
            pltpu.VMEM((hb, rows, rows), _F32),
        ],
        compiler_params=_params("arbitrary", "arbitrary", "arbitrary"),
        name="hgrn2",
    )(lb_logits, p, p, p, p, gain)


def _fox_decay_body(ff_ref, bias_ref, c_ref, carry_ref):
    @pl.when(pl.program_id(1) == 0)
    def _():
        carry_ref[...] = jnp.zeros_like(carry_ref)

    x = ff_ref[...] + bias_ref[...]
    logsig = jnp.minimum(x, 0.0) - jnp.log(1.0 + jnp.exp(-jnp.abs(x)))
    rows = x.shape[0]
    row = lax.broadcasted_iota(_I32, (rows, rows), 0)
    col = lax.broadcasted_iota(_I32, (rows, rows), 1)
    c = _dot_exact_lhs((row >= col).astype(_BF16), logsig) + carry_ref[...]
    c_ref[...] = c
    carry_ref[...] = c[rows - 1:rows, :]


def _fox_decay(ff, bias_pad, *, batch, seq):
    t = batch * seq
    rows = _pick(seq, (512, 256, 128))
    nblk = seq // rows
    return pl.pallas_call(
        _fox_decay_body,
        grid=(batch, nblk),
        in_specs=[
            pl.BlockSpec((rows, V7X_LANES), lambda b, n: (b * nblk + n, 0)),
            pl.BlockSpec((1, V7X_LANES), lambda b, n: (0, 0)),
        ],
        out_specs=pl.BlockSpec((rows, V7X_LANES), lambda b, n: (b * nblk + n, 0)),
        out_shape=jax.ShapeDtypeStruct((t, V7X_LANES), _F32),
        scratch_shapes=[pltpu.VMEM((1, V7X_LANES), _F32)],
        compiler_params=_params("arbitrary", "arbitrary"),
        name="fox_decay",
    )(ff, bias_pad)


def _fox_body(q_ref, k_ref, v_ref, c_ref, qg_ref, kg_ref, o_ref,
              ka_scr, qa_scr, m_scr, l_scr, acc_scr, s0_scr, s1_scr, p0_scr, p1_scr, *, tq, seq):
    h = pl.program_id(1)
    qi = pl.program_id(2)
    hd = q_ref.shape[2]
    lane = lax.broadcasted_iota(_I32, (tq, V7X_LANES), 1)

    def bias_lanes(c_block, sign, parts_first):
        c = jnp.sum(jnp.where(lane == h, c_block, 0.0), axis=-1, keepdims=True) * (sign * LOG2E)
        parts_at = 0 if parts_first else 3
        ones_at = 3 if parts_first else 0
        x = jnp.where((lane >= ones_at) & (lane < ones_at + 3), 1.0, 0.0)
        for j, part in enumerate(_split3(c)):
            x = jnp.where(lane == parts_at + j, part, x)
        return x.astype(_BF16)

    @pl.when(qi == 0)
    def _():
        def key_block(c, carry):
            r0 = pl.multiple_of(c * tq, tq)
            kk = k_ref[0, pl.ds(r0, tq), :].astype(_F32)
            ka_scr[pl.ds(r0, tq), :hd] = _rms(kk, kg_ref[...]).astype(_BF16)
            ka_scr[pl.ds(r0, tq), hd:] = bias_lanes(c_ref[pl.ds(r0, tq), :], -1.0, True)
            return carry
        lax.fori_loop(0, seq // tq, key_block, 0)

    q0 = pl.multiple_of(qi * tq, tq)
    qn = _rms(q_ref[0].astype(_F32), qg_ref[...]) * ((hd ** -0.5) * LOG2E)
    qa_scr[:, :hd] = qn.astype(_BF16)
    qa_scr[:, hd:] = bias_lanes(c_ref[pl.ds(q0, tq), :], 1.0, False)

    m_scr[...] = jnp.full_like(m_scr, NEG_BIG)
    l_scr[...] = jnp.zeros_like(l_scr)
    acc_scr[...] = jnp.zeros_like(acc_scr)

    s_bufs = (s0_scr, s1_scr)
    p_bufs = (p0_scr, p1_scr)

    def scores(j):
        r0 = pl.multiple_of(j * tq, tq)
        return lax.dot_general(qa_scr[...], ka_scr[pl.ds(r0, tq), :], (((1,), (1,)), ((), ())),
                               preferred_element_type=_F32)

    def softmax(s):
        m_old = m_scr[...]
        m_new = jnp.maximum(m_old, jnp.max(s, axis=-1, keepdims=True))
        alpha = jnp.exp2(m_old - m_new)
        p = jnp.exp2(s - jnp.tile(m_new, (1, tq // V7X_LANES)))
        l_scr[...] = alpha * l_scr[...] + jnp.sum(p, axis=-1, keepdims=True)
        m_scr[...] = m_new
        return alpha, p.astype(_BF16)

    def weighted_values(p, j):
        r0 = pl.multiple_of(j * tq, tq)
        return jnp.dot(p, v_ref[0, pl.ds(r0, tq), :], preferred_element_type=_F32)

    def full_block(j, buf):
        other = 1 - buf
        s_bufs[other][...] = scores(j + 1)
        alpha, p = softmax(s_bufs[buf][...])
        done = weighted_values(p_bufs[other][...], jnp.maximum(j - 1, 0))
        p_bufs[buf][...] = p
        acc_scr[...] = alpha * (acc_scr[...] + done)

    def diagonal_block(buf):
        row = lax.broadcasted_iota(_I32, (tq, tq), 0)
        col = lax.broadcasted_iota(_I32, (tq, tq), 1)
        alpha, p = softmax(jnp.where(col <= row, s_bufs[buf][...], NEG_BIG))
        done = weighted_values(p_bufs[1 - buf][...], jnp.maximum(qi - 1, 0))
        acc = alpha * (acc_scr[...] + done) + weighted_values(p, qi)
        o_ref[...] = (acc / l_scr[...]).astype(o_ref.dtype)

    p1_scr[...] = jnp.zeros_like(p1_scr)
    s0_scr[...] = scores(0)

    def block_pair(jj, carry):
        full_block(2 * jj, 0)
        full_block(2 * jj + 1, 1)
        return carry

    lax.fori_loop(0, qi // 2, block_pair, 0)

    @pl.when((qi & 1) == 0)
    def _():
        diagonal_block(0)

    @pl.when((qi & 1) == 1)
    def _():
        full_block(qi - 1, 0)
        diagonal_block(1)


def _fox_attn(p, c, q_gain, k_gain, *, batch, seq, heads, slab_q, slab_k, slab_v):
    t = batch * seq
    hd = p.shape[2]
    tq = _pick(seq, (512, 256))
    nq = seq // tq
    return pl.pallas_call(
        functools.partial(_fox_body, tq=tq, seq=seq),
        grid=(batch, heads, nq),
        in_specs=[
            pl.BlockSpec((1, tq, hd), lambda b, h, i: (slab_q + h, b * nq + i, 0)),
            pl.BlockSpec((1, seq, hd), lambda b, h, i: (slab_k + h, b, 0)),
            pl.BlockSpec((1, seq, hd), lambda b, h, i: (slab_v + h, b, 0)),
            pl.BlockSpec((seq, V7X_LANES), lambda b, h, i: (b, 0)),
            pl.BlockSpec((1, hd), lambda b, h, i: (0, 0)),
            pl.BlockSpec((1, hd), lambda b, h, i: (0, 0)),
        ],
        out_specs=pl.BlockSpec((tq, hd), lambda b, h, i: (b * nq + i, h)),
        out_shape=jax.ShapeDtypeStruct((t, heads * hd), _BF16),
        scratch_shapes=[
            pltpu.VMEM((seq, 2 * hd), _BF16),
            pltpu.VMEM((tq, 2 * hd), _BF16),
            pltpu.VMEM((tq, V7X_LANES), _F32),
            pltpu.VMEM((tq, V7X_LANES), _F32),
            pltpu.VMEM((tq, hd), _F32),
            pltpu.VMEM((tq, tq), _F32),
            pltpu.VMEM((tq, tq), _F32),
            pltpu.VMEM((tq, tq), _BF16),
            pltpu.VMEM((tq, tq), _BF16),
        ],
        compiler_params=_params("arbitrary", "arbitrary", "arbitrary"),
        name="fox_attn",
    )(p, p, p, c, q_gain, k_gain)


def _merge_body(a_ref, b_ref, ga_ref, gb_ref, x_ref, wa_ref, wb_ref, wo_ref, n2_ref, rw_ref,
                rb_ref, x2_ref, h2_ref, route_ref, gate_ref, cnt_ref, carry_ref, *, rpt):
    @pl.when(pl.program_id(0) == 0)
    def _():
        carry_ref[...] = jnp.zeros_like(carry_ref)

    def gates(ref):
        parts = [ref[s] for s in range(ref.shape[0])]
        g = parts[0] if len(parts) == 1 else jnp.concatenate(parts, axis=-1)
        return _sigmoid(g.astype(_F32))

    ua = jnp.dot(a_ref[...], wa_ref[...], preferred_element_type=_F32)
    ub = jnp.dot(b_ref[...], wb_ref[...], preferred_element_type=_F32)
    merged = gates(ga_ref) * ua + gates(gb_ref) * ub
    x2 = x_ref[...] + jnp.dot(merged.astype(_BF16), wo_ref[...], preferred_element_type=_F32)
    x2_ref[...] = x2
    h2 = _rms(x2, n2_ref[...])
    _store_token_tiles(h2_ref, h2, rpt)

    logits = _dot_f32(h2, rw_ref[...]) + rb_ref[...]
    tm = logits.shape[0]
    lane = lax.broadcasted_iota(_I32, logits.shape, 1)
    lane_f = lane.astype(_F32)
    work = logits
    top_v, top_i = [], []
    for _ in range(TOP_K):
        mx = jnp.max(work, axis=-1, keepdims=True)
        ix = jnp.min(jnp.where(work == mx, lane_f, float(V7X_LANES)), axis=-1,
                     keepdims=True).astype(_I32)
        top_v.append(mx)
        top_i.append(ix)
        work = jnp.where(lane == ix, NEG_BIG, work)

    ex = [jnp.exp(tv - top_v[0]) for tv in top_v]
    denom = ex[0]
    for t_ in ex[1:]:
        denom = denom + t_

    onehot = jnp.zeros(logits.shape, _F32)
    for ix in top_i:
        onehot = onehot + (lane == ix).astype(_F32)
    row = lax.broadcasted_iota(_I32, (tm, tm), 0)
    col = lax.broadcasted_iota(_I32, (tm, tm), 1)
    before = jnp.dot((row > col).astype(_BF16), onehot.astype(_BF16),
                     preferred_element_type=_F32) + carry_ref[...]

    route = jnp.zeros(logits.shape, _I32)
    gate = jnp.zeros(logits.shape, _F32)
    for r in range(TOP_K):
        rank = jnp.sum(jnp.where(lane == top_i[r], before, 0.0), axis=-1, keepdims=True)
        route = jnp.where(lane == r, top_i[r], route)
        route = jnp.where(lane == TOP_K + r, rank.astype(_I32), route)
        gate = jnp.where(lane == r, ex[r] / denom, gate)
    route_ref[...] = route
    gate_ref[...] = gate

    carry = carry_ref[...] + jnp.sum(onehot, axis=0, keepdims=True)
    carry_ref[...] = carry
    cnt_ref[...] = carry.astype(_I32)


def _merge(a, b, p, x2d, wa, wb, wo, n2, rw_pad, rb_pad, *, slab_ga, slab_gb, rpt):
    t, d = x2d.shape
    tm = _pick(t, (256, 128))
    gs = d // V7X_LANES
    const = lambda i: (0, 0)
    once = pl.Buffered(1)
    return pl.pallas_call(
        functools.partial(_merge_body, rpt=rpt),
        grid=(t // tm,),
        in_specs=[
            pl.BlockSpec((tm, a.shape[1]), lambda i: (i, 0)),
            pl.BlockSpec((tm, b.shape[1]), lambda i: (i, 0)),
            pl.BlockSpec((gs, tm, V7X_LANES), lambda i: (slab_ga // gs, i, 0)),
            pl.BlockSpec((gs, tm, V7X_LANES), lambda i: (slab_gb // gs, i, 0)),
            pl.BlockSpec((tm, d), lambda i: (i, 0)),
            pl.BlockSpec(wa.shape, const, pipeline_mode=once),
            pl.BlockSpec(wb.shape, const, pipeline_mode=once),
            pl.BlockSpec(wo.shape, const, pipeline_mode=once),
            pl.BlockSpec((1, d), const),
            pl.BlockSpec(rw_pad.shape, const, pipeline_mode=once),
            pl.BlockSpec((1, V7X_LANES), const),
        ],
        out_specs=[
            pl.BlockSpec((tm, d), lambda i: (i, 0)),
            pl.BlockSpec((tm * rpt, V7X_LANES), lambda i: (i, 0)),
            pl.BlockSpec((tm, V7X_LANES), lambda i: (i, 0)),
            pl.BlockSpec((tm, V7X_LANES), lambda i: (i, 0)),
            pl.BlockSpec((1, V7X_LANES), const),
        ],
        out_shape=[
            jax.ShapeDtypeStruct((t, d), _F32),
            jax.ShapeDtypeStruct((t * rpt, V7X_LANES), _U32),
            jax.ShapeDtypeStruct((t, V7X_LANES), _I32),
            jax.ShapeDtypeStruct((t, V7X_LANES), _F32),
            jax.ShapeDtypeStruct((1, V7X_LANES), _I32),
        ],
        scratch_shapes=[pltpu.VMEM((1, V7X_LANES), _F32)],
        compiler_params=_params("arbitrary"),
        name="merge_router",
    )(a, b, p, p, x2d, wa, wb, wo, n2, rw_pad, rb_pad)


def _start_and_wait_all(tokens, make_copy):
    assert DMA_UNROLL % TOP_K == 0 and tokens % (DMA_UNROLL // TOP_K) == 0
    group_tokens = DMA_UNROLL // TOP_K

    def group(i, do):
        for u in range(DMA_UNROLL):
            do(make_copy(i * DMA_UNROLL + u, i * group_tokens + u // TOP_K, u % TOP_K), u)

    def start(i, carry):
        group(i, lambda copy, u: copy.start(priority=u % 2))
        return carry

    def wait(i, carry):
        group(i, lambda copy, u: copy.wait())
        return carry

    lax.fori_loop(0, tokens // group_tokens, start, 0)
    lax.fori_loop(0, tokens // group_tokens, wait, 0)


def _tile_copy(src, src_tile, dst, dst_tile, sem, rpt, tiles=1):
    def first_row(tile):
        return tile * rpt if isinstance(tile, int) else pl.multiple_of(tile * rpt, rpt)

    n = tiles * rpt
    return pltpu.make_async_copy(src.at[pl.ds(first_row(src_tile), n)],
                                 dst.at[pl.ds(first_row(dst_tile), n)], sem)


def _dispatch_body(pstart_ref, cnt_ref, eidx_ref, rank_ref, h2_ref, xs_ref, zeros_scr,
                   sem, zsem, *, tokens, n_exp, n_rows, rpt):
    step = pl.program_id(0)

    def token_copy(a, token, choice):
        dst = pstart_ref[eidx_ref[a]] + rank_ref[a]
        return _tile_copy(h2_ref, token, xs_ref, dst, sem, rpt)

    def fill(do):
        def expert_pad(e, carry):
            first = pstart_ref[e] + cnt_ref[e]

            def pad_tile(c, carry):
                do(_tile_copy(zeros_scr, 0, xs_ref, first + c, zsem, rpt))
                return carry

            lax.fori_loop(0, pstart_ref[e + 1] - first, pad_tile, 0)
            return carry

        lax.fori_loop(0, n_exp, expert_pad, 0)

        tail = pstart_ref[n_exp]

        def tail_block(c, carry):
            do(_tile_copy(zeros_scr, 0, xs_ref, tail + c * EXPERT_ROWS, zsem, rpt, EXPERT_ROWS))
            return carry

        lax.fori_loop(0, (n_rows - tail) // EXPERT_ROWS, tail_block, 0)

    @pl.when(step == 0)
    def _():
        zeros_scr[...] = jnp.zeros_like(zeros_scr)
        fill(lambda c: c.start())

    _start_and_wait_all(tokens, token_copy)

    @pl.when(step == 0)
    def _():
        fill(lambda c: c.wait())


def _dispatch(h2, eidx, rank, pstart, counts, *, n_rows, rpt):
    t = h2.shape[0] // rpt
    tokens = _pick(t, (256, 128))
    n_exp = counts.shape[0]
    smem = lambda: pl.BlockSpec((tokens * TOP_K,), lambda i, *_: (i,), memory_space=pltpu.SMEM)
    return pl.pallas_call(
        functools.partial(_dispatch_body, tokens=tokens, n_exp=n_exp, n_rows=n_rows, rpt=rpt),
        grid_spec=pltpu.PrefetchScalarGridSpec(
            num_scalar_prefetch=2,
            grid=(t // tokens,),
            in_specs=[smem(), smem(),
                      pl.BlockSpec((tokens * rpt, V7X_LANES), lambda i, *_: (i, 0))],
            out_specs=pl.BlockSpec(memory_space=pl.ANY),
            scratch_shapes=[pltpu.VMEM((EXPERT_ROWS * rpt, V7X_LANES), _U32),
                            pltpu.SemaphoreType.DMA, pltpu.SemaphoreType.DMA],
        ),
        out_shape=jax.ShapeDtypeStruct((n_rows * rpt, V7X_LANES), _U32),
        compiler_params=_params("arbitrary"),
        name="dispatch",
    )(pstart, counts, eidx, rank, h2)


def _expert_block_state(bexp_ref, nvalid_ref):
    i = pl.program_id(1)
    valid = i < nvalid_ref[0]
    changed = jnp.logical_or(i == 0, bexp_ref[i] != bexp_ref[jnp.maximum(i - 1, 0)])
    return valid, jnp.logical_and(valid, changed)


def _expert_gu_body(bexp_ref, nvalid_ref, x_ref, wg_ref, wu_ref, bg_ref, bu_ref, o_ref, w_scr,
                    *, rpt):
    valid, changed = _expert_block_state(bexp_ref, nvalid_ref)

    @pl.when(changed)
    def _():
        w_scr[0] = wg_ref[0].astype(_BF16)
        w_scr[1] = wu_ref[0].astype(_BF16)

    @pl.when(valid)
    def _():
        slabs = _load_token_tiles(x_ref, o_ref.shape[0], rpt)
        x = jnp.concatenate([s.astype(_BF16) for s in slabs], axis=-1)
        gate = jnp.dot(x, w_scr[0], preferred_element_type=_F32) + bg_ref[0]
        up = jnp.dot(x, w_scr[1], preferred_element_type=_F32) + bu_ref[0]
        gate = jnp.minimum(gate, SWIGLU_LIMIT)
        up = jnp.clip(up, -SWIGLU_LIMIT, SWIGLU_LIMIT)
        o_ref[...] = ((up + 1.0) * gate * _sigmoid(SWIGLU_ALPHA * gate)).astype(o_ref.dtype)

    @pl.when(jnp.logical_not(valid))
    def _():
        o_ref[...] = jnp.zeros_like(o_ref)


def _block_index(i, nvalid_ref):
    return jnp.minimum(i, nvalid_ref[0] - 1)


def _expert_gu(xs, w_gu, b_gu, bexp, nvalid, *, rpt):
    n_rows = xs.shape[0] // rpt
    n_exp, d, two_de = w_gu.shape
    de = two_de // 2
    tm = EXPERT_ROWS
    tn = _pick(de, (1024, 512, 256, 128))
    nj = de // tn
    blk = _block_index
    return pl.pallas_call(
        functools.partial(_expert_gu_body, rpt=rpt),
        grid_spec=pltpu.PrefetchScalarGridSpec(
            num_scalar_prefetch=2,
            grid=(nj, n_rows // tm),
            in_specs=[
                pl.BlockSpec((tm * rpt, V7X_LANES), lambda j, i, be, nv: (blk(i, nv), 0)),
                pl.BlockSpec((1, d, tn), lambda j, i, be, nv: (be[blk(i, nv)], 0, j)),
                pl.BlockSpec((1, d, tn), lambda j, i, be, nv: (be[blk(i, nv)], 0, nj + j)),
                pl.BlockSpec((1, 1, tn), lambda j, i, be, nv: (be[blk(i, nv)], 0, j)),
                pl.BlockSpec((1, 1, tn), lambda j, i, be, nv: (be[blk(i, nv)], 0, nj + j)),
            ],
            out_specs=pl.BlockSpec((tm, tn), lambda j, i, be, nv: (i, j)),
            scratch_shapes=[pltpu.VMEM((2, d, tn), _BF16)],
        ),
        out_shape=jax.ShapeDtypeStruct((n_rows, de), _BF16),
        compiler_params=_params("arbitrary", "arbitrary"),
        name="expert_gu",
    )(bexp, nvalid, xs, w_gu, w_gu, b_gu.reshape(n_exp, 1, two_de), b_gu.reshape(n_exp, 1, two_de))


def _expert_down_body(bexp_ref, nvalid_ref, a_ref, w_ref, b_ref, o_ref, w_scr, *, rpt):
    valid, changed = _expert_block_state(bexp_ref, nvalid_ref)

    @pl.when(changed)
    def _():
        w_scr[...] = w_ref[0].astype(_BF16)

    @pl.when(valid)
    def _():
        out = jnp.dot(a_ref[...], w_scr[...], preferred_element_type=_F32) + b_ref[0]
        _store_token_tiles(o_ref, out, rpt)

    @pl.when(jnp.logical_not(valid))
    def _():
        o_ref[...] = jnp.zeros_like(o_ref)


def _expert_down(act, w_down, b_down, bexp, nvalid, *, rpt):
    n_rows, de = act.shape
    n_exp, _, d = w_down.shape
    tm = EXPERT_ROWS
    blk = _block_index
    return pl.pallas_call(
        functools.partial(_expert_down_body, rpt=rpt),
        grid_spec=pltpu.PrefetchScalarGridSpec(
            num_scalar_prefetch=2,
            grid=(1, n_rows // tm),
            in_specs=[
                pl.BlockSpec((tm, de), lambda j, i, be, nv: (blk(i, nv), 0)),
                pl.BlockSpec((1, de, d), lambda j, i, be, nv: (be[blk(i, nv)], 0, 0)),
                pl.BlockSpec((1, 1, d), lambda j, i, be, nv: (be[blk(i, nv)], 0, 0)),
            ],
            out_specs=pl.BlockSpec((tm * rpt, V7X_LANES), lambda j, i, be, nv: (i, 0)),
            scratch_shapes=[pltpu.VMEM((de, d), _BF16)],
        ),
        out_shape=jax.ShapeDtypeStruct((n_rows * rpt, V7X_LANES), _U32),
        compiler_params=_params("arbitrary", "arbitrary"),
        name="expert_down",
    )(bexp, nvalid, act, w_down, b_down.reshape(n_exp, 1, d))


def _combine_body(pstart_ref, eidx_ref, rank_ref, gate_ref, x2_ref, rows_ref, o_ref, buf, sem,
                  *, tokens, rpt):
    def gather(a, token, choice):
        src = pstart_ref[eidx_ref[a]] + rank_ref[a]
        return _tile_copy(rows_ref, src, buf, choice * tokens + token, sem, rpt)

    _start_and_wait_all(tokens, gather)

    gate = gate_ref[...]
    acc = [None] * (2 * rpt)
    for r in range(TOP_K):
        g = gate[:, r:r + 1]
        for s, slab in enumerate(_load_token_tiles(buf, tokens, rpt, first=r * tokens * rpt)):
            acc[s] = g * slab if acc[s] is None else acc[s] + g * slab
    o_ref[...] = x2_ref[...] + jnp.concatenate(acc, axis=-1)


def _combine(x2, rows, eidx, rank, gate, pstart, *, rpt):
    t, d = x2.shape
    tokens = _pick(t, (256, 128))
    smem = lambda: pl.BlockSpec((tokens * TOP_K,), lambda i, *_: (i,), memory_space=pltpu.SMEM)
    return pl.pallas_call(
        functools.partial(_combine_body, tokens=tokens, rpt=rpt),
        grid_spec=pltpu.PrefetchScalarGridSpec(
            num_scalar_prefetch=1,
            grid=(t // tokens,),
            in_specs=[smem(), smem(),
                      pl.BlockSpec((tokens, V7X_LANES), lambda i, *_: (i, 0)),
                      pl.BlockSpec((tokens, d), lambda i, *_: (i, 0)),
                      pl.BlockSpec(memory_space=pl.ANY)],
            out_specs=pl.BlockSpec((tokens, d), lambda i, *_: (i, 0)),
            scratch_shapes=[pltpu.VMEM((TOP_K * tokens * rpt, V7X_LANES), _U32),
                            pltpu.SemaphoreType.DMA],
        ),
        out_shape=jax.ShapeDtypeStruct((t, d), _F32),
        compiler_params=_params("arbitrary"),
        name="combine",
    )(pstart, eidx, rank, gate, x2, rows)


def _pad_lanes(v, fill=0.0):
    v = v.reshape(1, -1).astype(_F32)
    return jnp.pad(v, ((0, 0), (0, V7X_LANES - v.shape[1])), constant_values=fill)


def kernel(x, lb_logits, norm1, w_in, hg_norm, fox_q_norm, fox_k_norm, fox_f_bias, w_up_hg, w_up_fox,
           w_out, norm2, router_w, router_b, w_gu, b_gu, w_down, b_down):
    batch, seq, d = x.shape
    t = batch * seq
    depth = w_in.shape[0]
    hd = hg_norm.shape[1]
    hg_heads = w_up_hg.shape[1] // hd
    fox_heads = fox_f_bias.shape[1]
    hg_w, fox_w = hg_heads * hd, fox_heads * hd
    n_exp = router_w.shape[2]
    assert hd == V7X_LANES and fox_q_norm.shape[1] == hd and lb_logits.shape[1] == hg_w
    assert n_exp <= V7X_LANES and d % (2 * V7X_LANES) == 0
    assert w_in.shape[2] == 4 * hg_w + 3 * fox_w + fox_heads + 2 * d
    rpt = d // (2 * V7X_LANES)

    ff0 = 4 * hg_w + 3 * fox_w
    slab_ga = 0
    slab_gb = d // V7X_LANES
    slab_hq = 2 * (d // V7X_LANES)
    slab_fq = slab_hq + 4 * hg_heads

    n_rows = t * TOP_K + n_exp * EXPERT_ROWS
    xf = x.reshape(t, d)
    for l in range(depth):
        w = w_in[l]
        w_main = jnp.concatenate([w[:, ff0 + fox_heads:], w[:, :ff0]], axis=1).astype(_BF16)
        w_ff = jnp.pad(w[:, ff0:ff0 + fox_heads], ((0, 0), (0, V7X_LANES - fox_heads)))
        p, ff = _in_proj(xf, norm1[l].reshape(1, d), w_main, w_ff)

        a = _hgrn2(p, lb_logits, hg_norm[l].reshape(1, hd), layer=l, batch=batch, seq=seq,
                   heads=hg_heads, slab_q=slab_hq, slab_f=slab_hq + hg_heads,
                   slab_i=slab_hq + 2 * hg_heads, slab_g=slab_hq + 3 * hg_heads)

        c = _fox_decay(ff, _pad_lanes(fox_f_bias[l]), batch=batch, seq=seq)
        b = _fox_attn(p, c, fox_q_norm[l].reshape(1, hd), fox_k_norm[l].reshape(1, hd),
                      batch=batch, seq=seq, heads=fox_heads, slab_q=slab_fq,
                      slab_k=slab_fq + fox_heads, slab_v=slab_fq + 2 * fox_heads)

        rw_pad = jnp.pad(router_w[l], ((0, 0), (0, V7X_LANES - n_exp)))
        x2, h2, route, gate, counts = _merge(
            a, b, p, xf, w_up_hg[l].astype(_BF16), w_up_fox[l].astype(_BF16),
            w_out[l].astype(_BF16), norm2[l].reshape(1, d), rw_pad,
            _pad_lanes(router_b[l], NEG_BIG), slab_ga=slab_ga, slab_gb=slab_gb, rpt=rpt)

        counts = counts[0, :n_exp]
        padded = (counts + EXPERT_ROWS - 1) // EXPERT_ROWS * EXPERT_ROWS
        upto = jnp.arange(n_exp)[:, None] >= jnp.arange(n_exp)[None, :]
        pend = jnp.sum(jnp.where(upto, padded[None, :], 0), axis=1).astype(_I32)
        pstart = jnp.concatenate([jnp.zeros((1,), _I32), pend])
        n_blocks = n_rows // EXPERT_ROWS
        block_first = jnp.arange(n_blocks, dtype=_I32) * EXPERT_ROWS
        bexp = jnp.minimum(jnp.sum(pend[None, :] <= block_first[:, None], axis=1),
                           n_exp - 1).astype(_I32)
        nvalid = (pend[-1:] // EXPERT_ROWS).astype(_I32)
        eidx = route[:, :TOP_K].reshape(-1)
        rank = route[:, TOP_K:2 * TOP_K].reshape(-1)

        xs = _dispatch(h2, eidx, rank, pstart, counts, n_rows=n_rows, rpt=rpt)
        act = _expert_gu(xs, w_gu[l], b_gu[l], bexp, nvalid, rpt=rpt)
        rows = _expert_down(act, w_down[l], b_down[l], bexp, nvalid, rpt=rpt)
        xf = _combine(x2, rows, eidx, rank, gate, pstart, rpt=rpt)
    return xf.reshape(batch, seq, d)
```

```python
import functools

import jax
import jax.numpy as jnp
from jax import lax
from jax.experimental import pallas as pl
from jax.experimental.pallas import tpu as pltpu

_F32 = jnp.float32
_BF16 = jnp.bfloat16
_I32 = jnp.int32
_U32 = jnp.uint32

V7X_LANES = 128
V7X_SUBLANES = 8
V7X_VMEM_BYTES = 64 * 1024 * 1024
VMEM_LIMIT_BYTES = V7X_VMEM_BYTES * 7 // 8

EPS = 1e-6
SWIGLU_LIMIT = 7.0
SWIGLU_ALPHA = 1.702
TOP_K = 4
TOP_K_SHIFT = TOP_K.bit_length() - 1
assert 1 << TOP_K_SHIFT == TOP_K
NEG_BIG = -1e30
LOG2E = 1.4426950408889634

HGRN_ROWS = 256
HGRN_DIAG = V7X_SUBLANES
HGRN_HEADS = 8
EXPERT_ROWS = 512
DMA_UNROLL = 8


def _params(*semantics):
    return pltpu.CompilerParams(dimension_semantics=semantics, vmem_limit_bytes=VMEM_LIMIT_BYTES)


def _pick(n, candidates):
    for c in candidates:
        if n % c == 0:
            return c
    raise ValueError(f"no tile in {candidates} divides {n}")


def _sigmoid(x):
    return 1.0 / (1.0 + jnp.exp(-x))


def _split3(x):
    hi = x.astype(_BF16).astype(_F32)
    r = x - hi
    mid = r.astype(_BF16).astype(_F32)
    lo = (r - mid).astype(_BF16).astype(_F32)
    return hi, mid, lo


def _dot_exact_lhs(lhs_bf16, x):
    out = None
    for part in _split3(x):
        t = jnp.dot(lhs_bf16, part.astype(_BF16), preferred_element_type=_F32)
        out = t if out is None else out + t
    return out


def _dot_f32(a, b):
    a_hi, a_lo, _ = _split3(a)
    b_hi, b_lo, _ = _split3(b)
    a_hi, a_lo, b_hi, b_lo = (t.astype(_BF16) for t in (a_hi, a_lo, b_hi, b_lo))
    return (jnp.dot(a_hi, b_hi, preferred_element_type=_F32)
            + jnp.dot(a_hi, b_lo, preferred_element_type=_F32)
            + jnp.dot(a_lo, b_hi, preferred_element_type=_F32))


def _rms(x, gain):
    return x * lax.rsqrt(jnp.mean(x * x, axis=-1, keepdims=True) + EPS) * gain


def _pack_pair(lo, hi):
    lo_bits = lax.bitcast_convert_type(lo.astype(_BF16).astype(_F32), _U32)
    hi_bits = lax.bitcast_convert_type(hi.astype(_BF16).astype(_F32), _U32)
    return (lo_bits >> 16) | (hi_bits & jnp.uint32(0xFFFF0000))


def _unpack_pair(words):
    lo = lax.bitcast_convert_type(words << 16, _F32)
    hi = lax.bitcast_convert_type(words & jnp.uint32(0xFFFF0000), _F32)
    return lo, hi


def _store_token_tiles(ref, x, rpt):
    rows = x.shape[0]
    for k in range(rpt):
        lo = x[:, (2 * k) * V7X_LANES:(2 * k + 1) * V7X_LANES]
        hi = x[:, (2 * k + 1) * V7X_LANES:(2 * k + 2) * V7X_LANES]
        ref[pl.ds(k, rows, stride=rpt), :] = _pack_pair(lo, hi)


def _load_token_tiles(ref, rows, rpt, first=0):
    slabs = []
    for k in range(rpt):
        slabs.extend(_unpack_pair(ref[pl.ds(first + k, rows, stride=rpt), :]))
    return slabs


def _inproj_body(x_ref, g_ref, w_ref, wff_ref, p_ref, ff_ref, h_scr):
    @pl.when(pl.program_id(1) == 0)
    def _():
        h = _rms(x_ref[...], g_ref[...])
        h_scr[...] = h.astype(_BF16)
        ff_ref[...] = _dot_f32(h, wff_ref[...])

    acc = jnp.dot(h_scr[...], w_ref[...], preferred_element_type=_F32)
    for s in range(p_ref.shape[0]):
        p_ref[s] = acc[:, s * V7X_LANES:(s + 1) * V7X_LANES].astype(_BF16)


def _in_proj(x2d, gain, w_main, w_ff):
    t, d = x2d.shape
    n = w_main.shape[1]
    tm = _pick(t, (1024, 512, 256, 128))
    tn = _pick(n, (1024, 512, 256, 128))
    slabs = tn // V7X_LANES
    return pl.pallas_call(
        _inproj_body,
        grid=(t // tm, n // tn),
        in_specs=[
            pl.BlockSpec((tm, d), lambda i, j: (i, 0)),
            pl.BlockSpec((1, d), lambda i, j: (0, 0)),
            pl.BlockSpec((d, tn), lambda i, j: (0, j)),
            pl.BlockSpec((d, V7X_LANES), lambda i, j: (0, 0)),
        ],
        out_specs=[
            pl.BlockSpec((slabs, tm, V7X_LANES), lambda i, j: (j, i, 0)),
            pl.BlockSpec((tm, V7X_LANES), lambda i, j: (i, 0)),
        ],
        out_shape=[
            jax.ShapeDtypeStruct((n // V7X_LANES, t, V7X_LANES), _BF16),
            jax.ShapeDtypeStruct((t, V7X_LANES), _F32),
        ],
        scratch_shapes=[pltpu.VMEM((tm, d), _BF16)],
        compiler_params=_params("arbitrary", "arbitrary"),
        name="in_proj",
    )(x2d, gain, w_main, w_ff)


def _hgrn_body(lbl_ref, q_ref, f_ref, i_ref, g_ref, gain_ref, o_ref,
               st_ref, b_scr, q_scr, k_scr, d_scr, *, layer, rows, dk):
    @pl.when(pl.program_id(2) == 0)
    def _():
        st_ref[...] = jnp.zeros_like(st_ref)

    heads = range(q_ref.shape[0])
    for hb in heads:
        _hgrn_prepare(lbl_ref.at[:, pl.ds(hb * dk, dk)], q_ref.at[hb], f_ref.at[hb],
                      b_scr.at[hb], q_scr.at[hb], k_scr.at[hb], layer=layer, rows=rows, dk=dk)

    def diag_blocks(gi, carry):
        for hb in heads:
            _hgrn_diag_block(gi, b_scr.at[hb], q_scr.at[hb], k_scr.at[hb], d_scr.at[hb], rows)
        return carry

    lax.fori_loop(0, rows // HGRN_DIAG, diag_blocks, 0, unroll=4)

    for hb in heads:
        _hgrn_finish(i_ref.at[hb], g_ref.at[hb], gain_ref, o_ref.at[:, pl.ds(hb * dk, dk)],
                     st_ref.at[hb], b_scr.at[hb], q_scr.at[hb], k_scr.at[hb], d_scr.at[hb],
                     rows=rows)


def _hgrn_prepare(lbl_ref, q_ref, f_ref, b_scr, q_scr, k_scr, *, layer, rows, dk):
    lbl = lbl_ref[...]
    e = jnp.exp(lbl - jnp.max(lbl, axis=0, keepdims=True))
    lb = jnp.sum(e[:layer + 1], axis=0, keepdims=True) / jnp.sum(e, axis=0, keepdims=True)

    q = q_ref[...].astype(_F32)
    qf = q * _sigmoid(q) * (dk ** -0.5)
    fg = lb + (1.0 - lb) * _sigmoid(f_ref[...].astype(_F32))
    row = lax.broadcasted_iota(_I32, (rows, rows), 0)
    col = lax.broadcasted_iota(_I32, (rows, rows), 1)
    b_scr[...] = _dot_exact_lhs((row >= col).astype(_BF16), jnp.log(fg))
    q_scr[...] = qf
    k_scr[...] = 1.0 - fg


def _hgrn_diag_block(gi, b_scr, q_scr, k_scr, d_scr, rows):
    r0 = pl.multiple_of(gi * HGRN_DIAG, HGRN_DIAG)
    bb = b_scr[pl.ds(r0, HGRN_DIAG), :]
    qq = q_scr[pl.ds(r0, HGRN_DIAG), :]
    kk = k_scr[pl.ds(r0, HGRN_DIAG), :]
    sub = lax.broadcasted_iota(_I32, bb.shape, 0)
    colb = lax.broadcasted_iota(_I32, (HGRN_DIAG, rows), 1)
    acc = jnp.zeros((HGRN_DIAG, rows), _F32)
    for sl in range(HGRN_DIAG):
        decay = jnp.exp(jnp.where(sub >= sl, bb - bb[sl:sl + 1, :], NEG_BIG))
        sc = jnp.sum(qq * decay * kk[sl:sl + 1, :], axis=-1, keepdims=True)
        acc = jnp.where(colb == r0 + sl, sc, acc)
    d_scr[pl.ds(r0, HGRN_DIAG), :] = acc


def _hgrn_finish(i_ref, g_ref, gain_ref, o_ref, st_ref, b_scr, q_scr, k_scr, d_scr, *, rows):
    b = b_scr[...]
    qf = q_scr[...]
    k = k_scr[...]
    v = i_ref[...]
    row = lax.broadcasted_iota(_I32, (rows, rows), 0)
    col = lax.broadcasted_iota(_I32, (rows, rows), 1)

    scores = d_scr[...]
    rowv = lax.broadcasted_iota(_I32, (rows, V7X_LANES), 0)
    x = rows // 2
    while x >= HGRN_DIAG:
        mids = [jnp.broadcast_to(b[m + x - 1:m + x, :], (2 * x, b.shape[1]))
                for m in range(0, rows, 2 * x)]
        bm = mids[0] if len(mids) == 1 else jnp.concatenate(mids, axis=0)
        upper = (rowv & x) != 0
        w = jnp.exp(-jnp.abs(b - bm))
        qx = jnp.where(upper, qf * w, 0.0).astype(_BF16)
        kx = jnp.where(upper, 0.0, k * w).astype(_BF16)
        sx = lax.dot_general(qx, kx, (((1,), (1,)), ((), ())), preferred_element_type=_F32)
        if 2 * x < rows:
            shift = (2 * x).bit_length() - 1
            sx = jnp.where((row >> shift) == (col >> shift), sx, 0.0)
        scores = scores + sx
        x //= 2

    o = jnp.dot(scores.astype(_BF16), v, preferred_element_type=_F32)

    st = st_ref[...]
    q_in = (qf * jnp.exp(b)).astype(_BF16)
    o = o + lax.dot_general(q_in, st.astype(_BF16), (((1,), (1,)), ((), ())),
                            preferred_element_type=_F32)
    b_last = b[rows - 1:rows, :]
    k_out = (k * jnp.exp(b_last - b)).astype(_BF16)
    upd = lax.dot_general(v, k_out, (((0,), (0,)), ((), ())), preferred_element_type=_F32)
    st_ref[...] = st * jnp.exp(b_last) + upd

    g = g_ref[...].astype(_F32)
    o_ref[...] = (_rms(o, gain_ref[...]) * (g * _sigmoid(g))).astype(o_ref.dtype)


def _hgrn2(p, lb_logits, gain, *, layer, batch, seq, heads, slab_q, slab_f, slab_i, slab_g):
    t = batch * seq
    hd = p.shape[2]
    rows = min(HGRN_ROWS, seq)
    nblk = seq // rows

    hb = next(c for c in range(HGRN_HEADS, 0, -1)
              if heads % c == 0 and all(s % c == 0 for s in (slab_q, slab_f, slab_i, slab_g)))

    def slab(first):
        return pl.BlockSpec((hb, rows, hd), lambda b, h, n: (first // hb + h, b * nblk + n, 0))

    return pl.pallas_call(
        functools.partial(_hgrn_body, layer=layer, rows=rows, dk=hd),
        grid=(batch, heads // hb, nblk),
        in_specs=[
            pl.BlockSpec((lb_logits.shape[0], hb * hd), lambda b, h, n: (0, h)),
            slab(slab_q), slab(slab_f), slab(slab_i), slab(slab_g),
            pl.BlockSpec((1, hd), lambda b, h, n: (0, 0)),
        ],
        out_specs=pl.BlockSpec((rows, hb * hd), lambda b, h, n: (b * nblk + n, h)),
        out_shape=jax.ShapeDtypeStruct((t, heads * hd), _BF16),
        scratch_shapes=[
            pltpu.VMEM((hb, hd, hd), _F32),
            pltpu.VMEM((hb, rows, hd), _F32),
            pltpu.VMEM((hb, rows, hd), _F32),
            pltpu.VMEM((hb, rows, hd), _F32),
            pltpu.VMEM((hb, rows, rows), _F32),
        ],
        compiler_params=_params("arbitrary", "arbitrary", "arbitrary"),
        name="hgrn2",
    )(lb_logits, p, p, p, p, gain)


def _fox_decay_body(ff_ref, bias_ref, c_ref, carry_ref):
    @pl.when(pl.program_id(1) == 0)
    def _():
        carry_ref[...] = jnp.zeros_like(carry_ref)

    x = ff_ref[...] + bias_ref[...]
    logsig = jnp.minimum(x, 0.0) - jnp.log(1.0 + jnp.exp(-jnp.abs(x)))
    rows = x.shape[0]
    row = lax.broadcasted_iota(_I32, (rows, rows), 0)
    col = lax.broadcasted_iota(_I32, (rows, rows), 1)
    c = _dot_exact_lhs((row >= col).astype(_BF16), logsig) + carry_ref[...]
    c_ref[...] = c
    carry_ref[...] = c[rows - 1:rows, :]


def _fox_decay(ff, bias_pad, *, batch, seq):
    t = batch * seq
    rows = _pick(seq, (512, 256, 128))
    nblk = seq // rows
    return pl.pallas_call(
        _fox_decay_body,
        grid=(batch, nblk),
        in_specs=[
            pl.BlockSpec((rows, V7X_LANES), lambda b, n: (b * nblk + n, 0)),
            pl.BlockSpec((1, V7X_LANES), lambda b, n: (0, 0)),
        ],
        out_specs=pl.BlockSpec((rows, V7X_LANES), lambda b, n: (b * nblk + n, 0)),
        out_shape=jax.ShapeDtypeStruct((t, V7X_LANES), _F32),
        scratch_shapes=[pltpu.VMEM((1, V7X_LANES), _F32)],
        compiler_params=_params("arbitrary", "arbitrary"),
        name="fox_decay",
    )(ff, bias_pad)


def _fox_body(q_ref, k_ref, v_ref, c_ref, qg_ref, kg_ref, o_ref,
              ka_scr, vt_scr, qa_scr, m_scr, l_scr, acc_scr, s0_scr, s1_scr, p0_scr, p1_scr,
              *, tq, seq):
    h = pl.program_id(1)
    qi = pl.program_id(2)
    hd = q_ref.shape[2]
    lane = lax.broadcasted_iota(_I32, (tq, V7X_LANES), 1)

    def bias_lanes(c_block, sign, parts_first):
        c = jnp.sum(jnp.where(lane == h, c_block, 0.0), axis=-1, keepdims=True) * (sign * LOG2E)
        parts_at = 0 if parts_first else 3
        ones_at = 3 if parts_first else 0
        x = jnp.where((lane >= ones_at) & (lane < ones_at + 3), 1.0, 0.0)
        for j, part in enumerate(_split3(c)):
            x = jnp.where(lane == parts_at + j, part, x)
        return x.astype(_BF16)

    @pl.when(qi == 0)
    def _():
        def key_block(c, carry):
            r0 = pl.multiple_of(c * tq, tq)
            kk = k_ref[0, pl.ds(r0, tq), :].astype(_F32)
            ka_scr[pl.ds(r0, tq), :hd] = _rms(kk, kg_ref[...]).astype(_BF16)
            ka_scr[pl.ds(r0, tq), hd:] = bias_lanes(c_ref[pl.ds(r0, tq), :], -1.0, True)
            vt_scr[c] = jnp.transpose(v_ref[0, pl.ds(r0, tq), :].astype(_F32)).astype(_BF16)
            return carry
        lax.fori_loop(0, seq // tq, key_block, 0)

    q0 = pl.multiple_of(qi * tq, tq)
    qn = _rms(q_ref[0].astype(_F32), qg_ref[...]) * ((hd ** -0.5) * LOG2E)
    qa_scr[:, :hd] = qn.astype(_BF16)
    qa_scr[:, hd:] = bias_lanes(c_ref[pl.ds(q0, tq), :], 1.0, False)

    m_scr[...] = jnp.full_like(m_scr, NEG_BIG)
    l_scr[...] = jnp.zeros_like(l_scr)
    acc_scr[...] = jnp.zeros_like(acc_scr)

    s_bufs = (s0_scr, s1_scr)
    p_bufs = (p0_scr, p1_scr)
    last = seq // tq - 1

    def scores(j):
        r0 = pl.multiple_of(jnp.minimum(j, last) * tq, tq)
        return lax.dot_general(ka_scr[pl.ds(r0, tq), :], qa_scr[...], (((1,), (1,)), ((), ())),
                               preferred_element_type=_F32)

    def weighted_values(p, j):
        return jnp.dot(vt_scr[jnp.maximum(j, 0)], p, preferred_element_type=_F32)

    def stream_block(st, j, diagonal):
        s = s_bufs[st][...]
        if diagonal:
            key = lax.broadcasted_iota(_I32, (tq, tq), 0)
            query = lax.broadcasted_iota(_I32, (tq, tq), 1)
            s = jnp.where(key <= query, s, NEG_BIG)
        m_old = m_scr[st]
        m_new = jnp.maximum(m_old, jnp.max(s, axis=0, keepdims=True))
        alpha = jnp.exp2(m_old - m_new)
        p = jnp.exp2(s - m_new)
        l_scr[st] = alpha * l_scr[st] + jnp.sum(p, axis=0, keepdims=True)
        m_scr[st] = m_new
        done = weighted_values(p_bufs[st][...], j - 2)
        p_bufs[st][...] = p.astype(_BF16)
        acc_scr[st] = alpha * (acc_scr[st] + done)

    p0_scr[...] = jnp.zeros_like(p0_scr)
    p1_scr[...] = jnp.zeros_like(p1_scr)
    s0_scr[...] = scores(0)
    s1_scr[...] = scores(1)

    def block_pair(jj, carry):
        for st in range(2):
            stream_block(st, 2 * jj + st, False)
            s_bufs[st][...] = scores(2 * jj + 2 + st)
        return carry

    lax.fori_loop(0, qi // 2, block_pair, 0)

    @pl.when((qi & 1) == 0)
    def _():
        stream_block(0, qi, True)

    @pl.when((qi & 1) == 1)
    def _():
        stream_block(0, qi - 1, False)
        stream_block(1, qi, True)

    odd = qi & 1
    newest = (qi - odd, qi - 1 + odd)
    m = jnp.maximum(m_scr[0], m_scr[1])
    acc = jnp.zeros((hd, tq), _F32)
    l = jnp.zeros((1, tq), _F32)
    for st in range(2):
        w = jnp.exp2(m_scr[st] - m)
        acc = acc + w * (acc_scr[st] + weighted_values(p_bufs[st][...], newest[st]))
        l = l + w * l_scr[st]
    o_ref[...] = jnp.transpose(acc / l).astype(o_ref.dtype)


def _fox_attn(p, c, q_gain, k_gain, *, batch, seq, heads, slab_q, slab_k, slab_v):
    t = batch * seq
    hd = p.shape[2]
    tq = _pick(seq, (512, 256))
    nq = seq // tq
    return pl.pallas_call(
        functools.partial(_fox_body, tq=tq, seq=seq),
        grid=(batch, heads, nq),
        in_specs=[
            pl.BlockSpec((1, tq, hd), lambda b, h, i: (slab_q + h, b * nq + i, 0)),
            pl.BlockSpec((1, seq, hd), lambda b, h, i: (slab_k + h, b, 0)),
            pl.BlockSpec((1, seq, hd), lambda b, h, i: (slab_v + h, b, 0)),
            pl.BlockSpec((seq, V7X_LANES), lambda b, h, i: (b, 0)),
            pl.BlockSpec((1, hd), lambda b, h, i: (0, 0)),
            pl.BlockSpec((1, hd), lambda b, h, i: (0, 0)),
        ],
        out_specs=pl.BlockSpec((tq, hd), lambda b, h, i: (b * nq + i, h)),
        out_shape=jax.ShapeDtypeStruct((t, heads * hd), _BF16),
        scratch_shapes=[
            pltpu.VMEM((seq, 2 * hd), _BF16),
            pltpu.VMEM((nq, hd, tq), _BF16),
            pltpu.VMEM((tq, 2 * hd), _BF16),
            pltpu.VMEM((2, 1, tq), _F32),
            pltpu.VMEM((2, 1, tq), _F32),
            pltpu.VMEM((2, hd, tq), _F32),
            pltpu.VMEM((tq, tq), _F32),
            pltpu.VMEM((tq, tq), _F32),
            pltpu.VMEM((tq, tq), _BF16),
            pltpu.VMEM((tq, tq), _BF16),
        ],
        compiler_params=_params("arbitrary", "arbitrary", "arbitrary"),
        name="fox_attn",
    )(p, p, p, c, q_gain, k_gain)


def _merge_body(a_ref, b_ref, ga_ref, gb_ref, x_ref, wa_ref, wb_ref, wo_ref, n2_ref, rw_ref,
                rb_ref, x2_ref, h2_ref, route_ref, gate_ref, cnt_ref, carry_ref, *, rpt):
    @pl.when(pl.program_id(0) == 0)
    def _():
        carry_ref[...] = jnp.zeros_like(carry_ref)

    def gates(ref):
        parts = [ref[s] for s in range(ref.shape[0])]
        g = parts[0] if len(parts) == 1 else jnp.concatenate(parts, axis=-1)
        return _sigmoid(g.astype(_F32))

    ua = jnp.dot(a_ref[...], wa_ref[...], preferred_element_type=_F32)
    ub = jnp.dot(b_ref[...], wb_ref[...], preferred_element_type=_F32)
    merged = gates(ga_ref) * ua + gates(gb_ref) * ub
    x2 = x_ref[...] + jnp.dot(merged.astype(_BF16), wo_ref[...], preferred_element_type=_F32)
    x2_ref[...] = x2
    h2 = _rms(x2, n2_ref[...])
    _store_token_tiles(h2_ref, h2, rpt)

    logits = _dot_f32(h2, rw_ref[...]) + rb_ref[...]
    tm = logits.shape[0]
    lane = lax.broadcasted_iota(_I32, logits.shape, 1)
    lane_f = lane.astype(_F32)
    work = logits
    top_v, top_i = [], []
    for _ in range(TOP_K):
        mx = jnp.max(work, axis=-1, keepdims=True)
        ix = jnp.min(jnp.where(work == mx, lane_f, float(V7X_LANES)), axis=-1,
                     keepdims=True).astype(_I32)
        top_v.append(mx)
        top_i.append(ix)
        work = jnp.where(lane == ix, NEG_BIG, work)

    ex = [jnp.exp(tv - top_v[0]) for tv in top_v]
    denom = ex[0]
    for t_ in ex[1:]:
        denom = denom + t_

    onehot = jnp.zeros(logits.shape, _F32)
    for ix in top_i:
        onehot = onehot + (lane == ix).astype(_F32)
    row = lax.broadcasted_iota(_I32, (tm, tm), 0)
    col = lax.broadcasted_iota(_I32, (tm, tm), 1)
    before = jnp.dot((row > col).astype(_BF16), onehot.astype(_BF16),
                     preferred_element_type=_F32) + carry_ref[...]

    route = jnp.zeros(logits.shape, _I32)
    gate = jnp.zeros(logits.shape, _F32)
    for r in range(TOP_K):
        rank = jnp.sum(jnp.where(lane == top_i[r], before, 0.0), axis=-1, keepdims=True)
        route = jnp.where(lane == r, top_i[r], route)
        route = jnp.where(lane == TOP_K + r, rank.astype(_I32), route)
        gate = jnp.where(lane == r, ex[r] / denom, gate)
    route_ref[...] = route
    gate_ref[...] = gate

    carry = carry_ref[...] + jnp.sum(onehot, axis=0, keepdims=True)
    carry_ref[...] = carry
    cnt_ref[...] = carry.astype(_I32)


def _merge(a, b, p, x2d, wa, wb, wo, n2, rw_pad, rb_pad, *, slab_ga, slab_gb, rpt):
    t, d = x2d.shape
    tm = _pick(t, (256, 128))
    gs = d // V7X_LANES
    const = lambda i: (0, 0)
    once = pl.Buffered(1)
    return pl.pallas_call(
        functools.partial(_merge_body, rpt=rpt),
        grid=(t // tm,),
        in_specs=[
            pl.BlockSpec((tm, a.shape[1]), lambda i: (i, 0)),
            pl.BlockSpec((tm, b.shape[1]), lambda i: (i, 0)),
            pl.BlockSpec((gs, tm, V7X_LANES), lambda i: (slab_ga // gs, i, 0)),
            pl.BlockSpec((gs, tm, V7X_LANES), lambda i: (slab_gb // gs, i, 0)),
            pl.BlockSpec((tm, d), lambda i: (i, 0)),
            pl.BlockSpec(wa.shape, const, pipeline_mode=once),
            pl.BlockSpec(wb.shape, const, pipeline_mode=once),
            pl.BlockSpec(wo.shape, const, pipeline_mode=once),
            pl.BlockSpec((1, d), const),
            pl.BlockSpec(rw_pad.shape, const, pipeline_mode=once),
            pl.BlockSpec((1, V7X_LANES), const),
        ],
        out_specs=[
            pl.BlockSpec((tm, d), lambda i: (i, 0)),
            pl.BlockSpec((tm * rpt, V7X_LANES), lambda i: (i, 0)),
            pl.BlockSpec((tm, V7X_LANES), lambda i: (i, 0)),
            pl.BlockSpec((tm, V7X_LANES), lambda i: (i, 0)),
            pl.BlockSpec((1, V7X_LANES), const),
        ],
        out_shape=[
            jax.ShapeDtypeStruct((t, d), _F32),
            jax.ShapeDtypeStruct((t * rpt, V7X_LANES), _U32),
            jax.ShapeDtypeStruct((t, V7X_LANES), _I32),
            jax.ShapeDtypeStruct((t, V7X_LANES), _F32),
            jax.ShapeDtypeStruct((1, V7X_LANES), _I32),
        ],
        scratch_shapes=[pltpu.VMEM((1, V7X_LANES), _F32)],
        compiler_params=_params("arbitrary"),
        name="merge_router",
    )(a, b, p, p, x2d, wa, wb, wo, n2, rw_pad, rb_pad)


def _start_and_wait_all(tokens, make_copy):
    assert DMA_UNROLL % TOP_K == 0 and tokens % (DMA_UNROLL // TOP_K) == 0
    group_tokens = DMA_UNROLL // TOP_K

    def group(i, do):
        for u in range(DMA_UNROLL):
            do(make_copy(i * DMA_UNROLL + u, i * group_tokens + u // TOP_K, u % TOP_K), u)

    def start(i, carry):
        group(i, lambda copy, u: copy.start(priority=u % 2))
        return carry

    def wait(i, carry):
        group(i, lambda copy, u: copy.wait())
        return carry

    lax.fori_loop(0, tokens // group_tokens, start, 0)
    lax.fori_loop(0, tokens // group_tokens, wait, 0)


def _tile_copy(src, src_tile, dst, dst_tile, sem, rpt, tiles=1):
    def first_row(tile):
        return tile * rpt if isinstance(tile, int) else pl.multiple_of(tile * rpt, rpt)

    n = tiles * rpt
    return pltpu.make_async_copy(src.at[pl.ds(first_row(src_tile), n)],
                                 dst.at[pl.ds(first_row(dst_tile), n)], sem)


def _dispatch_body(pstart_ref, cnt_ref, eidx_ref, rank_ref, h2_ref, xs_ref, zeros_scr,
                   sem, zsem, *, tokens, n_exp, n_rows, rpt):
    step = pl.program_id(0)

    def token_copy(a, token, choice):
        dst = pstart_ref[eidx_ref[a]] + rank_ref[a]
        return _tile_copy(h2_ref, token, xs_ref, dst, sem, rpt)

    def fill(do):
        def expert_pad(e, carry):
            first = pstart_ref[e] + cnt_ref[e]

            def pad_tile(c, carry):
                do(_tile_copy(zeros_scr, 0, xs_ref, first + c, zsem, rpt))
                return carry

            lax.fori_loop(0, pstart_ref[e + 1] - first, pad_tile, 0)
            return carry

        lax.fori_loop(0, n_exp, expert_pad, 0)

        tail = pstart_ref[n_exp]

        def tail_block(c, carry):
            do(_tile_copy(zeros_scr, 0, xs_ref, tail + c * EXPERT_ROWS, zsem, rpt, EXPERT_ROWS))
            return carry

        lax.fori_loop(0, (n_rows - tail) // EXPERT_ROWS, tail_block, 0)

    @pl.when(step == 0)
    def _():
        zeros_scr[...] = jnp.zeros_like(zeros_scr)
        fill(lambda c: c.start())

    _start_and_wait_all(tokens, token_copy)

    @pl.when(step == 0)
    def _():
        fill(lambda c: c.wait())


def _dispatch(h2, eidx, rank, pstart, counts, *, n_rows, rpt):
    t = h2.shape[0] // rpt
    tokens = _pick(t, (256, 128))
    n_exp = counts.shape[0]
    smem = lambda: pl.BlockSpec((tokens * TOP_K,), lambda i, *_: (i,), memory_space=pltpu.SMEM)
    return pl.pallas_call(
        functools.partial(_dispatch_body, tokens=tokens, n_exp=n_exp, n_rows=n_rows, rpt=rpt),
        grid_spec=pltpu.PrefetchScalarGridSpec(
            num_scalar_prefetch=2,
            grid=(t // tokens,),
            in_specs=[smem(), smem(),
                      pl.BlockSpec((tokens * rpt, V7X_LANES), lambda i, *_: (i, 0))],
            out_specs=pl.BlockSpec(memory_space=pl.ANY),
            scratch_shapes=[pltpu.VMEM((EXPERT_ROWS * rpt, V7X_LANES), _U32),
                            pltpu.SemaphoreType.DMA, pltpu.SemaphoreType.DMA],
        ),
        out_shape=jax.ShapeDtypeStruct((n_rows * rpt, V7X_LANES), _U32),
        compiler_params=_params("arbitrary"),
        name="dispatch",
    )(pstart, counts, eidx, rank, h2)


def _expert_block_state(bexp_ref, nvalid_ref):
    i = pl.program_id(1)
    valid = i < nvalid_ref[0]
    changed = jnp.logical_or(i == 0, bexp_ref[i] != bexp_ref[jnp.maximum(i - 1, 0)])
    return valid, jnp.logical_and(valid, changed)


def _expert_gu_body(bexp_ref, nvalid_ref, x_ref, wg_ref, wu_ref, bg_ref, bu_ref, o_ref, w_scr,
                    *, rpt):
    valid, changed = _expert_block_state(bexp_ref, nvalid_ref)

    @pl.when(changed)
    def _():
        w_scr[0] = wg_ref[0].astype(_BF16)
        w_scr[1] = wu_ref[0].astype(_BF16)

    @pl.when(valid)
    def _():
        slabs = _load_token_tiles(x_ref, o_ref.shape[0], rpt)
        x = jnp.concatenate([s.astype(_BF16) for s in slabs], axis=-1)
        gate = jnp.dot(x, w_scr[0], preferred_element_type=_F32) + bg_ref[0]
        up = jnp.dot(x, w_scr[1], preferred_element_type=_F32) + bu_ref[0]
        gate = jnp.minimum(gate, SWIGLU_LIMIT)
        up = jnp.clip(up, -SWIGLU_LIMIT, SWIGLU_LIMIT)
        o_ref[...] = ((up + 1.0) * gate * _sigmoid(SWIGLU_ALPHA * gate)).astype(o_ref.dtype)

    @pl.when(jnp.logical_not(valid))
    def _():
        o_ref[...] = jnp.zeros_like(o_ref)


def _block_index(i, nvalid_ref):
    return jnp.minimum(i, nvalid_ref[0] - 1)


def _expert_gu(xs, w_gu, b_gu, bexp, nvalid, *, rpt):
    n_rows = xs.shape[0] // rpt
    n_exp, d, two_de = w_gu.shape
    de = two_de // 2
    tm = EXPERT_ROWS
    tn = _pick(de, (1024, 512, 256, 128))
    nj = de // tn
    blk = _block_index
    return pl.pallas_call(
        functools.partial(_expert_gu_body, rpt=rpt),
        grid_spec=pltpu.PrefetchScalarGridSpec(
            num_scalar_prefetch=2,
            grid=(nj, n_rows // tm),
            in_specs=[
                pl.BlockSpec((tm * rpt, V7X_LANES), lambda j, i, be, nv: (blk(i, nv), 0)),
                pl.BlockSpec((1, d, tn), lambda j, i, be, nv: (be[blk(i, nv)], 0, j)),
                pl.BlockSpec((1, d, tn), lambda j, i, be, nv: (be[blk(i, nv)], 0, nj + j)),
                pl.BlockSpec((1, 1, tn), lambda j, i, be, nv: (be[blk(i, nv)], 0, j)),
                pl.BlockSpec((1, 1, tn), lambda j, i, be, nv: (be[blk(i, nv)], 0, nj + j)),
            ],
            out_specs=pl.BlockSpec((tm, tn), lambda j, i, be, nv: (i, j)),
            scratch_shapes=[pltpu.VMEM((2, d, tn), _BF16)],
        ),
        out_shape=jax.ShapeDtypeStruct((n_rows, de), _BF16),
        compiler_params=_params("arbitrary", "arbitrary"),
        name="expert_gu",
    )(bexp, nvalid, xs, w_gu, w_gu, b_gu.reshape(n_exp, 1, two_de), b_gu.reshape(n_exp, 1, two_de))


def _expert_down_body(bexp_ref, nvalid_ref, a_ref, w_ref, b_ref, o_ref, w_scr, *, rpt):
    valid, changed = _expert_block_state(bexp_ref, nvalid_ref)

    @pl.when(changed)
    def _():
        w_scr[...] = w_ref[0].astype(_BF16)

    @pl.when(valid)
    def _():
        out = jnp.dot(a_ref[...], w_scr[...], preferred_element_type=_F32) + b_ref[0]
        _store_token_tiles(o_ref, out, rpt)

    @pl.when(jnp.logical_not(valid))
    def _():
        o_ref[...] = jnp.zeros_like(o_ref)


def _expert_down(act, w_down, b_down, bexp, nvalid, *, rpt):
    n_rows, de = act.shape
    n_exp, _, d = w_down.shape
    tm = EXPERT_ROWS
    blk = _block_index
    return pl.pallas_call(
        functools.partial(_expert_down_body, rpt=rpt),
        grid_spec=pltpu.PrefetchScalarGridSpec(
            num_scalar_prefetch=2,
            grid=(1, n_rows // tm),
            in_specs=[
                pl.BlockSpec((tm, de), lambda j, i, be, nv: (blk(i, nv), 0)),
                pl.BlockSpec((1, de, d), lambda j, i, be, nv: (be[blk(i, nv)], 0, 0)),
                pl.BlockSpec((1, 1, d), lambda j, i, be, nv: (be[blk(i, nv)], 0, 0)),
            ],
            out_specs=pl.BlockSpec((tm * rpt, V7X_LANES), lambda j, i, be, nv: (i, 0)),
            scratch_shapes=[pltpu.VMEM((de, d), _BF16)],
        ),
        out_shape=jax.ShapeDtypeStruct((n_rows * rpt, V7X_LANES), _U32),
        compiler_params=_params("arbitrary", "arbitrary"),
        name="expert_down",
    )(bexp, nvalid, act, w_down, b_down.reshape(n_exp, 1, d))


def _combine_body(pstart_ref, eidx_ref, rank_ref, gate_ref, x2_ref, rows_ref, o_ref, buf, sem,
                  *, tokens, rpt):
    def gather(a, token, choice):
        src = pstart_ref[eidx_ref[a]] + rank_ref[a]
        return _tile_copy(rows_ref, src, buf, choice * tokens + token, sem, rpt)

    _start_and_wait_all(tokens, gather)

    gate = gate_ref[...]
    acc = [None] * (2 * rpt)
    for r in range(TOP_K):
        g = gate[:, r:r + 1]
        for s, slab in enumerate(_load_token_tiles(buf, tokens, rpt, first=r * tokens * rpt)):
            acc[s] = g * slab if acc[s] is None else acc[s] + g * slab
    o_ref[...] = x2_ref[...] + jnp.concatenate(acc, axis=-1)


def _combine(x2, rows, eidx, rank, gate, pstart, *, rpt):
    t, d = x2.shape
    tokens = _pick(t, (256, 128))
    smem = lambda: pl.BlockSpec((tokens * TOP_K,), lambda i, *_: (i,), memory_space=pltpu.SMEM)
    return pl.pallas_call(
        functools.partial(_combine_body, tokens=tokens, rpt=rpt),
        grid_spec=pltpu.PrefetchScalarGridSpec(
            num_scalar_prefetch=1,
            grid=(t // tokens,),
            in_specs=[smem(), smem(),
                      pl.BlockSpec((tokens, V7X_LANES), lambda i, *_: (i, 0)),
                      pl.BlockSpec((tokens, d), lambda i, *_: (i, 0)),
                      pl.BlockSpec(memory_space=pl.ANY)],
            out_specs=pl.BlockSpec((tokens, d), lambda i, *_: (i, 0)),
            scratch_shapes=[pltpu.VMEM((TOP_K * tokens * rpt, V7X_LANES), _U32),
                            pltpu.SemaphoreType.DMA],
        ),
        out_shape=jax.ShapeDtypeStruct((t, d), _F32),
        compiler_params=_params("arbitrary"),
        name="combine",
    )(pstart, eidx, rank, gate, x2, rows)


def _pad_lanes(v, fill=0.0):
    v = v.reshape(1, -1).astype(_F32)
    return jnp.pad(v, ((0, 0), (0, V7X_LANES - v.shape[1])), constant_values=fill)


def kernel(x, lb_logits, norm1, w_in, hg_norm, fox_q_norm, fox_k_norm, fox_f_bias, w_up_hg, w_up_fox,
           w_out, norm2, router_w, router_b, w_gu, b_gu, w_down, b_down):
    batch, seq, d = x.shape
    t = batch * seq
    depth = w_in.shape[0]
    hd = hg_norm.shape[1]
    hg_heads = w_up_hg.shape[1] // hd
    fox_heads = fox_f_bias.shape[1]
    hg_w, fox_w = hg_heads * hd, fox_heads * hd
    n_exp = router_w.shape[2]
    assert hd == V7X_LANES and fox_q_norm.shape[1] == hd and lb_logits.shape[1] == hg_w
    assert n_exp <= V7X_LANES and d % (2 * V7X_LANES) == 0
    assert w_in.shape[2] == 4 * hg_w + 3 * fox_w + fox_heads + 2 * d
    rpt = d // (2 * V7X_LANES)

    ff0 = 4 * hg_w + 3 * fox_w
    slab_ga = 0
    slab_gb = d // V7X_LANES
    slab_hq = 2 * (d // V7X_LANES)
    slab_fq = slab_hq + 4 * hg_heads

    n_rows = t * TOP_K + n_exp * EXPERT_ROWS
    xf = x.reshape(t, d)
    for l in range(depth):
        w = w_in[l]
        w_main = jnp.concatenate([w[:, ff0 + fox_heads:], w[:, :ff0]], axis=1).astype(_BF16)
        w_ff = jnp.pad(w[:, ff0:ff0 + fox_heads], ((0, 0), (0, V7X_LANES - fox_heads)))
        p, ff = _in_proj(xf, norm1[l].reshape(1, d), w_main, w_ff)

        a = _hgrn2(p, lb_logits, hg_norm[l].reshape(1, hd), layer=l, batch=batch, seq=seq,
                   heads=hg_heads, slab_q=slab_hq, slab_f=slab_hq + hg_heads,
                   slab_i=slab_hq + 2 * hg_heads, slab_g=slab_hq + 3 * hg_heads)

        c = _fox_decay(ff, _pad_lanes(fox_f_bias[l]), batch=batch, seq=seq)
        b = _fox_attn(p, c, fox_q_norm[l].reshape(1, hd), fox_k_norm[l].reshape(1, hd),
                      batch=batch, seq=seq, heads=fox_heads, slab_q=slab_fq,
                      slab_k=slab_fq + fox_heads, slab_v=slab_fq + 2 * fox_heads)

        rw_pad = jnp.pad(router_w[l], ((0, 0), (0, V7X_LANES - n_exp)))
        x2, h2, route, gate, counts = _merge(
            a, b, p, xf, w_up_hg[l].astype(_BF16), w_up_fox[l].astype(_BF16),
            w_out[l].astype(_BF16), norm2[l].reshape(1, d), rw_pad,
            _pad_lanes(router_b[l], NEG_BIG), slab_ga=slab_ga, slab_gb=slab_gb, rpt=rpt)

        counts = counts[0, :n_exp]
        padded = (counts + EXPERT_ROWS - 1) // EXPERT_ROWS * EXPERT_ROWS
        upto = jnp.arange(n_exp)[:, None] >= jnp.arange(n_exp)[None, :]
        pend = jnp.sum(jnp.where(upto, padded[None, :], 0), axis=1).astype(_I32)
        pstart = jnp.concatenate([jnp.zeros((1,), _I32), pend])
        n_blocks = n_rows // EXPERT_ROWS
        block_first = jnp.arange(n_blocks, dtype=_I32) * EXPERT_ROWS
        bexp = jnp.minimum(jnp.sum(pend[None, :] <= block_first[:, None], axis=1),
                           n_exp - 1).astype(_I32)
        nvalid = (pend[-1:] // EXPERT_ROWS).astype(_I32)
        eidx = route[:, :TOP_K].reshape(-1)
        rank = route[:, TOP_K:2 * TOP_K].reshape(-1)

        xs = _dispatch(h2, eidx, rank, pstart, counts, n_rows=n_rows, rpt=rpt)
        act = _expert_gu(xs, w_gu[l], b_gu[l], bexp, nvalid, rpt=rpt)
        rows = _expert_down(act, w_down[l], b_down[l], bexp, nvalid, rpt=rpt)
        xf = _combine(x2, rows, eidx, rank, gate, pstart, rpt=rpt)
    return xf.reshape(batch, seq, d)
```

```python
import functools

import jax
import jax.numpy as jnp
from jax import lax
from jax.experimental import pallas as pl
from jax.experimental.pallas import tpu as pltpu

_F32 = jnp.float32
_BF16 = jnp.bfloat16
_I32 = jnp.int32
_U32 = jnp.uint32

V7X_LANES = 128
V7X_SUBLANES = 8
V7X_VMEM_BYTES = 64 * 1024 * 1024
VMEM_LIMIT_BYTES = V7X_VMEM_BYTES * 7 // 8

EPS = 1e-6
SWIGLU_LIMIT = 7.0
SWIGLU_ALPHA = 1.702
TOP_K = 4
TOP_K_SHIFT = TOP_K.bit_length() - 1
assert 1 << TOP_K_SHIFT == TOP_K
NEG_BIG = -1e30
LOG2E = 1.4426950408889634

HGRN_ROWS = 256
HGRN_DIAG = V7X_SUBLANES
HGRN_HEADS = 8
EXPERT_ROWS = 512
DMA_UNROLL = 8


def _params(*semantics):
    return pltpu.CompilerParams(dimension_semantics=semantics, vmem_limit_bytes=VMEM_LIMIT_BYTES)


def _pick(n, candidates):
    for c in candidates:
        if n % c == 0:
            return c
    raise ValueError(f"no tile in {candidates} divides {n}")


def _sigmoid(x):
    return 1.0 / (1.0 + jnp.exp(-x))


def _split3(x):
    hi = x.astype(_BF16).astype(_F32)
    r = x - hi
    mid = r.astype(_BF16).astype(_F32)
    lo = (r - mid).astype(_BF16).astype(_F32)
    return hi, mid, lo


def _dot_exact_lhs(lhs_bf16, x):
    out = None
    for part in _split3(x):
        t = jnp.dot(lhs_bf16, part.astype(_BF16), preferred_element_type=_F32)
        out = t if out is None else out + t
    return out


def _dot_f32(a, b):
    a_hi, a_lo, _ = _split3(a)
    b_hi, b_lo, _ = _split3(b)
    a_hi, a_lo, b_hi, b_lo = (t.astype(_BF16) for t in (a_hi, a_lo, b_hi, b_lo))
    return (jnp.dot(a_hi, b_hi, preferred_element_type=_F32)
            + jnp.dot(a_hi, b_lo, preferred_element_type=_F32)
            + jnp.dot(a_lo, b_hi, preferred_element_type=_F32))


def _rms(x, gain):
    return x * lax.rsqrt(jnp.mean(x * x, axis=-1, keepdims=True) + EPS) * gain


def _pack_pair(lo, hi):
    lo_bits = lax.bitcast_convert_type(lo.astype(_BF16).astype(_F32), _U32)
    hi_bits = lax.bitcast_convert_type(hi.astype(_BF16).astype(_F32), _U32)
    return (lo_bits >> 16) | (hi_bits & jnp.uint32(0xFFFF0000))


def _unpack_pair(words):
    lo = lax.bitcast_convert_type(words << 16, _F32)
    hi = lax.bitcast_convert_type(words & jnp.uint32(0xFFFF0000), _F32)
    return lo, hi


def _store_token_tiles(ref, x, rpt):
    rows = x.shape[0]
    for k in range(rpt):
        lo = x[:, (2 * k) * V7X_LANES:(2 * k + 1) * V7X_LANES]
        hi = x[:, (2 * k + 1) * V7X_LANES:(2 * k + 2) * V7X_LANES]
        ref[pl.ds(k, rows, stride=rpt), :] = _pack_pair(lo, hi)


def _load_token_tiles(ref, rows, rpt, first=0):
    slabs = []
    for k in range(rpt):
        slabs.extend(_unpack_pair(ref[pl.ds(first + k, rows, stride=rpt), :]))
    return slabs


def _inproj_body(x_ref, g_ref, w_ref, wff_ref, p_ref, ff_ref, h_scr):
    @pl.when(pl.program_id(1) == 0)
    def _():
        h = _rms(x_ref[...], g_ref[...])
        h_scr[...] = h.astype(_BF16)
        ff_ref[...] = _dot_f32(h, wff_ref[...])

    acc = jnp.dot(h_scr[...], w_ref[...], preferred_element_type=_F32)
    for s in range(p_ref.shape[0]):
        p_ref[s] = acc[:, s * V7X_LANES:(s + 1) * V7X_LANES].astype(_BF16)


def _in_proj(x2d, gain, w_main, w_ff):
    t, d = x2d.shape
    n = w_main.shape[1]
    tm = _pick(t, (1024, 512, 256, 128))
    tn = _pick(n, (1024, 512, 256, 128))
    slabs = tn // V7X_LANES
    return pl.pallas_call(
        _inproj_body,
        grid=(t // tm, n // tn),
        in_specs=[
            pl.BlockSpec((tm, d), lambda i, j: (i, 0)),
            pl.BlockSpec((1, d), lambda i, j: (0, 0)),
            pl.BlockSpec((d, tn), lambda i, j: (0, j)),
            pl.BlockSpec((d, V7X_LANES), lambda i, j: (0, 0)),
        ],
        out_specs=[
            pl.BlockSpec((slabs, tm, V7X_LANES), lambda i, j: (j, i, 0)),
            pl.BlockSpec((tm, V7X_LANES), lambda i, j: (i, 0)),
        ],
        out_shape=[
            jax.ShapeDtypeStruct((n // V7X_LANES, t, V7X_LANES), _BF16),
            jax.ShapeDtypeStruct((t, V7X_LANES), _F32),
        ],
        scratch_shapes=[pltpu.VMEM((tm, d), _BF16)],
        compiler_params=_params("arbitrary", "arbitrary"),
        name="in_proj",
    )(x2d, gain, w_main, w_ff)


def _hgrn_body(lbl_ref, q_ref, f_ref, i_ref, g_ref, gain_ref, o_ref,
               st_ref, b_scr, q_scr, k_scr, d_scr, *, layer, rows, dk):
    @pl.when(pl.program_id(2) == 0)
    def _():
        st_ref[...] = jnp.zeros_like(st_ref)

    heads = range(q_ref.shape[0])
    for hb in heads:
        _hgrn_prepare(lbl_ref.at[:, pl.ds(hb * dk, dk)], q_ref.at[hb], f_ref.at[hb],
                      b_scr.at[hb], q_scr.at[hb], k_scr.at[hb], layer=layer, rows=rows, dk=dk)

    def diag_blocks(gi, carry):
        for hb in heads:
            _hgrn_diag_block(gi, b_scr.at[hb], q_scr.at[hb], k_scr.at[hb], d_scr.at[hb], rows)
        return carry

    lax.fori_loop(0, rows // HGRN_DIAG, diag_blocks, 0, unroll=4)

    for hb in heads:
        _hgrn_finish(i_ref.at[hb], g_ref.at[hb], gain_ref, o_ref.at[:, pl.ds(hb * dk, dk)],
                     st_ref.at[hb], b_scr.at[hb], q_scr.at[hb], k_scr.at[hb], d_scr.at[hb],
                     rows=rows)


def _hgrn_prepare(lbl_ref, q_ref, f_ref, b_scr, q_scr, k_scr, *, layer, rows, dk):
    lbl = lbl_ref[...]
    e = jnp.exp(lbl - jnp.max(lbl, axis=0, keepdims=True))
    lb = jnp.sum(e[:layer + 1], axis=0, keepdims=True) / jnp.sum(e, axis=0, keepdims=True)

    q = q_ref[...].astype(_F32)
    qf = q * _sigmoid(q) * (dk ** -0.5)
    fg = lb + (1.0 - lb) * _sigmoid(f_ref[...].astype(_F32))
    row = lax.broadcasted_iota(_I32, (rows, rows), 0)
    col = lax.broadcasted_iota(_I32, (rows, rows), 1)
    b_scr[...] = _dot_exact_lhs((row >= col).astype(_BF16), jnp.log(fg))
    q_scr[...] = qf
    k_scr[...] = 1.0 - fg


def _hgrn_diag_block(gi, b_scr, q_scr, k_scr, d_scr, rows):
    r0 = pl.multiple_of(gi * HGRN_DIAG, HGRN_DIAG)
    bb = b_scr[pl.ds(r0, HGRN_DIAG), :]
    qq = q_scr[pl.ds(r0, HGRN_DIAG), :]
    kk = k_scr[pl.ds(r0, HGRN_DIAG), :]
    sub = lax.broadcasted_iota(_I32, bb.shape, 0)
    colb = lax.broadcasted_iota(_I32, (HGRN_DIAG, rows), 1)
    acc = jnp.zeros((HGRN_DIAG, rows), _F32)
    for sl in range(HGRN_DIAG):
        decay = jnp.exp(jnp.where(sub >= sl, bb - bb[sl:sl + 1, :], NEG_BIG))
        sc = jnp.sum(qq * decay * kk[sl:sl + 1, :], axis=-1, keepdims=True)
        acc = jnp.where(colb == r0 + sl, sc, acc)
    d_scr[pl.ds(r0, HGRN_DIAG), :] = acc


def _hgrn_finish(i_ref, g_ref, gain_ref, o_ref, st_ref, b_scr, q_scr, k_scr, d_scr, *, rows):
    b = b_scr[...]
    qf = q_scr[...]
    k = k_scr[...]
    v = i_ref[...]
    row = lax.broadcasted_iota(_I32, (rows, rows), 0)
    col = lax.broadcasted_iota(_I32, (rows, rows), 1)

    scores = d_scr[...]
    rowv = lax.broadcasted_iota(_I32, (rows, V7X_LANES), 0)
    x = rows // 2
    while x >= HGRN_DIAG:
        mids = [jnp.broadcast_to(b[m + x - 1:m + x, :], (2 * x, b.shape[1]))
                for m in range(0, rows, 2 * x)]
        bm = mids[0] if len(mids) == 1 else jnp.concatenate(mids, axis=0)
        upper = (rowv & x) != 0
        w = jnp.exp(-jnp.abs(b - bm))
        qx = jnp.where(upper, qf * w, 0.0).astype(_BF16)
        kx = jnp.where(upper, 0.0, k * w).astype(_BF16)
        sx = lax.dot_general(qx, kx, (((1,), (1,)), ((), ())), preferred_element_type=_F32)
        if 2 * x < rows:
            shift = (2 * x).bit_length() - 1
            sx = jnp.where((row >> shift) == (col >> shift), sx, 0.0)
        scores = scores + sx
        x //= 2

    o = jnp.dot(scores.astype(_BF16), v, preferred_element_type=_F32)

    st = st_ref[...]
    q_in = (qf * jnp.exp(b)).astype(_BF16)
    o = o + lax.dot_general(q_in, st.astype(_BF16), (((1,), (1,)), ((), ())),
                            preferred_element_type=_F32)
    b_last = b[rows - 1:rows, :]
    k_out = (k * jnp.exp(b_last - b)).astype(_BF16)
    upd = lax.dot_general(v, k_out, (((0,), (0,)), ((), ())), preferred_element_type=_F32)
    st_ref[...] = st * jnp.exp(b_last) + upd

    g = g_ref[...].astype(_F32)
    o_ref[...] = (_rms(o, gain_ref[...]) * (g * _sigmoid(g))).astype(o_ref.dtype)


def _hgrn2(p, lb_logits, gain, *, layer, batch, seq, heads, slab_q, slab_f, slab_i, slab_g):
    t = batch * seq
    hd = p.shape[2]
    rows = min(HGRN_ROWS, seq)
    nblk = seq // rows

    hb = next(c for c in range(HGRN_HEADS, 0, -1)
              if heads % c == 0 and all(s % c == 0 for s in (slab_q, slab_f, slab_i, slab_g)))

    def slab(first):
        return pl.BlockSpec((hb, rows, hd), lambda b, h, n: (first // hb + h, b * nblk + n, 0))

    return pl.pallas_call(
        functools.partial(_hgrn_body, layer=layer, rows=rows, dk=hd),
        grid=(batch, heads // hb, nblk),
        in_specs=[
            pl.BlockSpec((lb_logits.shape[0], hb * hd), lambda b, h, n: (0, h)),
            slab(slab_q), slab(slab_f), slab(slab_i), slab(slab_g),
            pl.BlockSpec((1, hd), lambda b, h, n: (0, 0)),
        ],
        out_specs=pl.BlockSpec((rows, hb * hd), lambda b, h, n: (b * nblk + n, h)),
        out_shape=jax.ShapeDtypeStruct((t, heads * hd), _BF16),
        scratch_shapes=[
            pltpu.VMEM((hb, hd, hd), _F32),
            pltpu.VMEM((hb, rows, hd), _F32),
            pltpu.VMEM((hb, rows, hd), _F32),
            pltpu.VMEM((hb, rows, hd), _F32),
            pltpu.VMEM((hb, rows, rows), _F32),
        ],
        compiler_params=_params("arbitrary", "arbitrary", "arbitrary"),
        name="hgrn2",
    )(lb_logits, p, p, p, p, gain)


def _fox_decay_body(ff_ref, bias_ref, c_ref, carry_ref):
    @pl.when(pl.program_id(1) == 0)
    def _():
        carry_ref[...] = jnp.zeros_like(carry_ref)

    x = ff_ref[...] + bias_ref[...]
    logsig = jnp.minimum(x, 0.0) - jnp.log(1.0 + jnp.exp(-jnp.abs(x)))
    rows = x.shape[0]
    row = lax.broadcasted_iota(_I32, (rows, rows), 0)
    col = lax.broadcasted_iota(_I32, (rows, rows), 1)
    c = _dot_exact_lhs((row >= col).astype(_BF16), logsig) + carry_ref[...]
    c_ref[...] = c
    carry_ref[...] = c[rows - 1:rows, :]


def _fox_decay(ff, bias_pad, *, batch, seq):
    t = batch * seq
    rows = _pick(seq, (512, 256, 128))
    nblk = seq // rows
    return pl.pallas_call(
        _fox_decay_body,
        grid=(batch, nblk),
        in_specs=[
            pl.BlockSpec((rows, V7X_LANES), lambda b, n: (b * nblk + n, 0)),
            pl.BlockSpec((1, V7X_LANES), lambda b, n: (0, 0)),
        ],
        out_specs=pl.BlockSpec((rows, V7X_LANES), lambda b, n: (b * nblk + n, 0)),
        out_shape=jax.ShapeDtypeStruct((t, V7X_LANES), _F32),
        scratch_shapes=[pltpu.VMEM((1, V7X_LANES), _F32)],
        compiler_params=_params("arbitrary", "arbitrary"),
        name="fox_decay",
    )(ff, bias_pad)


def _fox_body(q_ref, k_ref, v_ref, c_ref, qg_ref, kg_ref, o_ref,
              ka_scr, qa_scr, m_scr, l_scr, acc_scr, s0_scr, s1_scr, p0_scr, p1_scr, *, tq, seq):
    h = pl.program_id(1)
    qi = pl.program_id(2)
    hd = q_ref.shape[2]
    lane = lax.broadcasted_iota(_I32, (tq, V7X_LANES), 1)

    def bias_lanes(c_block, sign, parts_first):
        c = jnp.sum(jnp.where(lane == h, c_block, 0.0), axis=-1, keepdims=True) * (sign * LOG2E)
        parts_at = 0 if parts_first else 3
        ones_at = 3 if parts_first else 0
        x = jnp.where((lane >= ones_at) & (lane < ones_at + 3), 1.0, 0.0)
        for j, part in enumerate(_split3(c)):
            x = jnp.where(lane == parts_at + j, part, x)
        return x.astype(_BF16)

    @pl.when(qi == 0)
    def _():
        def key_block(c, carry):
            r0 = pl.multiple_of(c * tq, tq)
            kk = k_ref[0, pl.ds(r0, tq), :].astype(_F32)
            ka_scr[pl.ds(r0, tq), :hd] = _rms(kk, kg_ref[...]).astype(_BF16)
            ka_scr[pl.ds(r0, tq), hd:] = bias_lanes(c_ref[pl.ds(r0, tq), :], -1.0, True)
            return carry
        lax.fori_loop(0, seq // tq, key_block, 0)

    q0 = pl.multiple_of(qi * tq, tq)
    qn = _rms(q_ref[0].astype(_F32), qg_ref[...]) * ((hd ** -0.5) * LOG2E)
    qa_scr[:, :hd] = qn.astype(_BF16)
    qa_scr[:, hd:] = bias_lanes(c_ref[pl.ds(q0, tq), :], 1.0, False)

    m_scr[...] = jnp.full_like(m_scr, NEG_BIG)
    l_scr[...] = jnp.zeros_like(l_scr)
    acc_scr[...] = jnp.zeros_like(acc_scr)

    s_bufs = (s0_scr, s1_scr)
    p_bufs = (p0_scr, p1_scr)
    last = seq // tq - 1

    def scores(j):
        r0 = pl.multiple_of(jnp.minimum(j, last) * tq, tq)
        return lax.dot_general(qa_scr[...], ka_scr[pl.ds(r0, tq), :], (((1,), (1,)), ((), ())),
                               preferred_element_type=_F32)

    def weighted_values(p, j):
        r0 = pl.multiple_of(jnp.maximum(j, 0) * tq, tq)
        return jnp.dot(p, v_ref[0, pl.ds(r0, tq), :], preferred_element_type=_F32)

    def stream_block(st, j, diagonal):
        s = s_bufs[st][...]
        if diagonal:
            row = lax.broadcasted_iota(_I32, (tq, tq), 0)
            col = lax.broadcasted_iota(_I32, (tq, tq), 1)
            s = jnp.where(col <= row, s, NEG_BIG)
        m_old = m_scr[st]
        m_new = jnp.maximum(m_old, jnp.max(s, axis=-1, keepdims=True))
        alpha = jnp.exp2(m_old - m_new)
        p = jnp.exp2(s - jnp.tile(m_new, (1, tq // V7X_LANES)))
        l_scr[st] = alpha * l_scr[st] + jnp.sum(p, axis=-1, keepdims=True)
        m_scr[st] = m_new
        done = weighted_values(p_bufs[st][...], j - 2)
        p_bufs[st][...] = p.astype(_BF16)
        acc_scr[st] = alpha * (acc_scr[st] + done)

    p0_scr[...] = jnp.zeros_like(p0_scr)
    p1_scr[...] = jnp.zeros_like(p1_scr)
    s0_scr[...] = scores(0)
    s1_scr[...] = scores(1)

    def block_pair(jj, carry):
        for st in range(2):
            stream_block(st, 2 * jj + st, False)
            s_bufs[st][...] = scores(2 * jj + 2 + st)
        return carry

    lax.fori_loop(0, qi // 2, block_pair, 0)

    @pl.when((qi & 1) == 0)
    def _():
        stream_block(0, qi, True)

    @pl.when((qi & 1) == 1)
    def _():
        stream_block(0, qi - 1, False)
        stream_block(1, qi, True)

    odd = qi & 1
    newest = (qi - odd, qi - 1 + odd)
    m = jnp.maximum(m_scr[0], m_scr[1])
    acc = jnp.zeros((tq, hd), _F32)
    l = jnp.zeros((tq, V7X_LANES), _F32)
    for st in range(2):
        w = jnp.exp2(m_scr[st] - m)
        acc = acc + w * (acc_scr[st] + weighted_values(p_bufs[st][...], newest[st]))
        l = l + w * l_scr[st]
    o_ref[...] = (acc / l).astype(o_ref.dtype)


def _fox_attn(p, c, q_gain, k_gain, *, batch, seq, heads, slab_q, slab_k, slab_v):
    t = batch * seq
    hd = p.shape[2]
    tq = _pick(seq, (512, 256))
    nq = seq // tq
    return pl.pallas_call(
        functools.partial(_fox_body, tq=tq, seq=seq),
        grid=(batch, heads, nq),
        in_specs=[
            pl.BlockSpec((1, tq, hd), lambda b, h, i: (slab_q + h, b * nq + i, 0)),
            pl.BlockSpec((1, seq, hd), lambda b, h, i: (slab_k + h, b, 0)),
            pl.BlockSpec((1, seq, hd), lambda b, h, i: (slab_v + h, b, 0)),
            pl.BlockSpec((seq, V7X_LANES), lambda b, h, i: (b, 0)),
            pl.BlockSpec((1, hd), lambda b, h, i: (0, 0)),
            pl.BlockSpec((1, hd), lambda b, h, i: (0, 0)),
        ],
        out_specs=pl.BlockSpec((tq, hd), lambda b, h, i: (b * nq + i, h)),
        out_shape=jax.ShapeDtypeStruct((t, heads * hd), _BF16),
        scratch_shapes=[
            pltpu.VMEM((seq, 2 * hd), _BF16),
            pltpu.VMEM((tq, 2 * hd), _BF16),
            pltpu.VMEM((2, tq, V7X_LANES), _F32),
            pltpu.VMEM((2, tq, V7X_LANES), _F32),
            pltpu.VMEM((2, tq, hd), _F32),
            pltpu.VMEM((tq, tq), _F32),
            pltpu.VMEM((tq, tq), _F32),
            pltpu.VMEM((tq, tq), _BF16),
            pltpu.VMEM((tq, tq), _BF16),
        ],
        compiler_params=_params("arbitrary", "arbitrary", "arbitrary"),
        name="fox_attn",
    )(p, p, p, c, q_gain, k_gain)


def _merge_body(a_ref, b_ref, ga_ref, gb_ref, x_ref, wa_ref, wb_ref, wo_ref, n2_ref, rw_ref,
                rb_ref, x2_ref, h2_ref, route_ref, gate_ref, cnt_ref, carry_ref, *, rpt):
    @pl.when(pl.program_id(0) == 0)
    def _():
        carry_ref[...] = jnp.zeros_like(carry_ref)

    def gates(ref):
        parts = [ref[s] for s in range(ref.shape[0])]
        g = parts[0] if len(parts) == 1 else jnp.concatenate(parts, axis=-1)
        return _sigmoid(g.astype(_F32))

    ua = jnp.dot(a_ref[...], wa_ref[...], preferred_element_type=_F32)
    ub = jnp.dot(b_ref[...], wb_ref[...], preferred_element_type=_F32)
    merged = gates(ga_ref) * ua + gates(gb_ref) * ub
    x2 = x_ref[...] + jnp.dot(merged.astype(_BF16), wo_ref[...], preferred_element_type=_F32)
    x2_ref[...] = x2
    h2 = _rms(x2, n2_ref[...])
    _store_token_tiles(h2_ref, h2, rpt)

    logits = _dot_f32(h2, rw_ref[...]) + rb_ref[...]
    tm = logits.shape[0]
    lane = lax.broadcasted_iota(_I32, logits.shape, 1)
    lane_f = lane.astype(_F32)
    work = logits
    top_v, top_i = [], []
    for _ in range(TOP_K):
        mx = jnp.max(work, axis=-1, keepdims=True)
        ix = jnp.min(jnp.where(work == mx, lane_f, float(V7X_LANES)), axis=-1,
                     keepdims=True).astype(_I32)
        top_v.append(mx)
        top_i.append(ix)
        work = jnp.where(lane == ix, NEG_BIG, work)

    ex = [jnp.exp(tv - top_v[0]) for tv in top_v]
    denom = ex[0]
    for t_ in ex[1:]:
        denom = denom + t_

    onehot = jnp.zeros(logits.shape, _F32)
    for ix in top_i:
        onehot = onehot + (lane == ix).astype(_F32)
    row = lax.broadcasted_iota(_I32, (tm, tm), 0)
    col = lax.broadcasted_iota(_I32, (tm, tm), 1)
    before = jnp.dot((row > col).astype(_BF16), onehot.astype(_BF16),
                     preferred_element_type=_F32) + carry_ref[...]

    route = jnp.zeros(logits.shape, _I32)
    gate = jnp.zeros(logits.shape, _F32)
    for r in range(TOP_K):
        rank = jnp.sum(jnp.where(lane == top_i[r], before, 0.0), axis=-1, keepdims=True)
        route = jnp.where(lane == r, top_i[r], route)
        route = jnp.where(lane == TOP_K + r, rank.astype(_I32), route)
        gate = jnp.where(lane == r, ex[r] / denom, gate)
    route_ref[...] = route
    gate_ref[...] = gate

    carry = carry_ref[...] + jnp.sum(onehot, axis=0, keepdims=True)
    carry_ref[...] = carry
    cnt_ref[...] = carry.astype(_I32)


def _merge(a, b, p, x2d, wa, wb, wo, n2, rw_pad, rb_pad, *, slab_ga, slab_gb, rpt):
    t, d = x2d.shape
    tm = _pick(t, (256, 128))
    gs = d // V7X_LANES
    const = lambda i: (0, 0)
    once = pl.Buffered(1)
    return pl.pallas_call(
        functools.partial(_merge_body, rpt=rpt),
        grid=(t // tm,),
        in_specs=[
            pl.BlockSpec((tm, a.shape[1]), lambda i: (i, 0)),
            pl.BlockSpec((tm, b.shape[1]), lambda i: (i, 0)),
            pl.BlockSpec((gs, tm, V7X_LANES), lambda i: (slab_ga // gs, i, 0)),
            pl.BlockSpec((gs, tm, V7X_LANES), lambda i: (slab_gb // gs, i, 0)),
            pl.BlockSpec((tm, d), lambda i: (i, 0)),
            pl.BlockSpec(wa.shape, const, pipeline_mode=once),
            pl.BlockSpec(wb.shape, const, pipeline_mode=once),
            pl.BlockSpec(wo.shape, const, pipeline_mode=once),
            pl.BlockSpec((1, d), const),
            pl.BlockSpec(rw_pad.shape, const, pipeline_mode=once),
            pl.BlockSpec((1, V7X_LANES), const),
        ],
        out_specs=[
            pl.BlockSpec((tm, d), lambda i: (i, 0)),
            pl.BlockSpec((tm * rpt, V7X_LANES), lambda i: (i, 0)),
            pl.BlockSpec((tm, V7X_LANES), lambda i: (i, 0)),
            pl.BlockSpec((tm, V7X_LANES), lambda i: (i, 0)),
            pl.BlockSpec((1, V7X_LANES), const),
        ],
        out_shape=[
            jax.ShapeDtypeStruct((t, d), _F32),
            jax.ShapeDtypeStruct((t * rpt, V7X_LANES), _U32),
            jax.ShapeDtypeStruct((t, V7X_LANES), _I32),
            jax.ShapeDtypeStruct((t, V7X_LANES), _F32),
            jax.ShapeDtypeStruct((1, V7X_LANES), _I32),
        ],
        scratch_shapes=[pltpu.VMEM((1, V7X_LANES), _F32)],
        compiler_params=_params("arbitrary"),
        name="merge_router",
    )(a, b, p, p, x2d, wa, wb, wo, n2, rw_pad, rb_pad)


def _start_and_wait_all(tokens, make_copy):
    assert DMA_UNROLL % TOP_K == 0 and tokens % (DMA_UNROLL // TOP_K) == 0
    group_tokens = DMA_UNROLL // TOP_K

    def group(i, do):
        for u in range(DMA_UNROLL):
            do(make_copy(i * DMA_UNROLL + u, i * group_tokens + u // TOP_K, u % TOP_K), u)

    def start(i, carry):
        group(i, lambda copy, u: copy.start(priority=u % 2))
        return carry

    def wait(i, carry):
        group(i, lambda copy, u: copy.wait())
        return carry

    lax.fori_loop(0, tokens // group_tokens, start, 0)
    lax.fori_loop(0, tokens // group_tokens, wait, 0)


def _tile_copy(src, src_tile, dst, dst_tile, sem, rpt, tiles=1):
    def first_row(tile):
        return tile * rpt if isinstance(tile, int) else pl.multiple_of(tile * rpt, rpt)

    n = tiles * rpt
    return pltpu.make_async_copy(src.at[pl.ds(first_row(src_tile), n)],
                                 dst.at[pl.ds(first_row(dst_tile), n)], sem)


def _dispatch_body(pstart_ref, cnt_ref, eidx_ref, rank_ref, h2_ref, xs_ref, zeros_scr,
                   sem, zsem, *, tokens, n_exp, n_rows, rpt):
    step = pl.program_id(0)

    def token_copy(a, token, choice):
        dst = pstart_ref[eidx_ref[a]] + rank_ref[a]
        return _tile_copy(h2_ref, token, xs_ref, dst, sem, rpt)

    def fill(do):
        def expert_pad(e, carry):
            first = pstart_ref[e] + cnt_ref[e]

            def pad_tile(c, carry):
                do(_tile_copy(zeros_scr, 0, xs_ref, first + c, zsem, rpt))
                return carry

            lax.fori_loop(0, pstart_ref[e + 1] - first, pad_tile, 0)
            return carry

        lax.fori_loop(0, n_exp, expert_pad, 0)

        tail = pstart_ref[n_exp]

        def tail_block(c, carry):
            do(_tile_copy(zeros_scr, 0, xs_ref, tail + c * EXPERT_ROWS, zsem, rpt, EXPERT_ROWS))
            return carry

        lax.fori_loop(0, (n_rows - tail) // EXPERT_ROWS, tail_block, 0)

    @pl.when(step == 0)
    def _():
        zeros_scr[...] = jnp.zeros_like(zeros_scr)
        fill(lambda c: c.start())

    _start_and_wait_all(tokens, token_copy)

    @pl.when(step == 0)
    def _():
        fill(lambda c: c.wait())


def _dispatch(h2, eidx, rank, pstart, counts, *, n_rows, rpt):
    t = h2.shape[0] // rpt
    tokens = _pick(t, (256, 128))
    n_exp = counts.shape[0]
    smem = lambda: pl.BlockSpec((tokens * TOP_K,), lambda i, *_: (i,), memory_space=pltpu.SMEM)
    return pl.pallas_call(
        functools.partial(_dispatch_body, tokens=tokens, n_exp=n_exp, n_rows=n_rows, rpt=rpt),
        grid_spec=pltpu.PrefetchScalarGridSpec(
            num_scalar_prefetch=2,
            grid=(t // tokens,),
            in_specs=[smem(), smem(),
                      pl.BlockSpec((tokens * rpt, V7X_LANES), lambda i, *_: (i, 0))],
            out_specs=pl.BlockSpec(memory_space=pl.ANY),
            scratch_shapes=[pltpu.VMEM((EXPERT_ROWS * rpt, V7X_LANES), _U32),
                            pltpu.SemaphoreType.DMA, pltpu.SemaphoreType.DMA],
        ),
        out_shape=jax.ShapeDtypeStruct((n_rows * rpt, V7X_LANES), _U32),
        compiler_params=_params("arbitrary"),
        name="dispatch",
    )(pstart, counts, eidx, rank, h2)


def _expert_block_state(bexp_ref, nvalid_ref):
    i = pl.program_id(1)
    valid = i < nvalid_ref[0]
    changed = jnp.logical_or(i == 0, bexp_ref[i] != bexp_ref[jnp.maximum(i - 1, 0)])
    return valid, jnp.logical_and(valid, changed)


def _expert_gu_body(bexp_ref, nvalid_ref, x_ref, wg_ref, wu_ref, bg_ref, bu_ref, o_ref, w_scr,
                    *, rpt):
    valid, changed = _expert_block_state(bexp_ref, nvalid_ref)

    @pl.when(changed)
    def _():
        w_scr[0] = wg_ref[0].astype(_BF16)
        w_scr[1] = wu_ref[0].astype(_BF16)

    @pl.when(valid)
    def _():
        slabs = _load_token_tiles(x_ref, o_ref.shape[0], rpt)
        x = jnp.concatenate([s.astype(_BF16) for s in slabs], axis=-1)
        gate = jnp.dot(x, w_scr[0], preferred_element_type=_F32) + bg_ref[0]
        up = jnp.dot(x, w_scr[1], preferred_element_type=_F32) + bu_ref[0]
        gate = jnp.minimum(gate, SWIGLU_LIMIT)
        up = jnp.clip(up, -SWIGLU_LIMIT, SWIGLU_LIMIT)
        o_ref[...] = ((up + 1.0) * gate * _sigmoid(SWIGLU_ALPHA * gate)).astype(o_ref.dtype)

    @pl.when(jnp.logical_not(valid))
    def _():
        o_ref[...] = jnp.zeros_like(o_ref)


def _block_index(i, nvalid_ref):
    return jnp.minimum(i, nvalid_ref[0] - 1)


def _expert_gu(xs, w_gu, b_gu, bexp, nvalid, *, rpt):
    n_rows = xs.shape[0] // rpt
    n_exp, d, two_de = w_gu.shape
    de = two_de // 2
    tm = EXPERT_ROWS
    tn = _pick(de, (1024, 512, 256, 128))
    nj = de // tn
    blk = _block_index
    return pl.pallas_call(
        functools.partial(_expert_gu_body, rpt=rpt),
        grid_spec=pltpu.PrefetchScalarGridSpec(
            num_scalar_prefetch=2,
            grid=(nj, n_rows // tm),
            in_specs=[
                pl.BlockSpec((tm * rpt, V7X_LANES), lambda j, i, be, nv: (blk(i, nv), 0)),
                pl.BlockSpec((1, d, tn), lambda j, i, be, nv: (be[blk(i, nv)], 0, j)),
                pl.BlockSpec((1, d, tn), lambda j, i, be, nv: (be[blk(i, nv)], 0, nj + j)),
                pl.BlockSpec((1, 1, tn), lambda j, i, be, nv: (be[blk(i, nv)], 0, j)),
                pl.BlockSpec((1, 1, tn), lambda j, i, be, nv: (be[blk(i, nv)], 0, nj + j)),
            ],
            out_specs=pl.BlockSpec((tm, tn), lambda j, i, be, nv: (i, j)),
            scratch_shapes=[pltpu.VMEM((2, d, tn), _BF16)],
        ),
        out_shape=jax.ShapeDtypeStruct((n_rows, de), _BF16),
        compiler_params=_params("arbitrary", "arbitrary"),
        name="expert_gu",
    )(bexp, nvalid, xs, w_gu, w_gu, b_gu.reshape(n_exp, 1, two_de), b_gu.reshape(n_exp, 1, two_de))


def _expert_down_body(bexp_ref, nvalid_ref, a_ref, w_ref, b_ref, o_ref, w_scr, *, rpt):
    valid, changed = _expert_block_state(bexp_ref, nvalid_ref)

    @pl.when(changed)
    def _():
        w_scr[...] = w_ref[0].astype(_BF16)

    @pl.when(valid)
    def _():
        out = jnp.dot(a_ref[...], w_scr[...], preferred_element_type=_F32) + b_ref[0]
        _store_token_tiles(o_ref, out, rpt)

    @pl.when(jnp.logical_not(valid))
    def _():
        o_ref[...] = jnp.zeros_like(o_ref)


def _expert_down(act, w_down, b_down, bexp, nvalid, *, rpt):
    n_rows, de = act.shape
    n_exp, _, d = w_down.shape
    tm = EXPERT_ROWS
    blk = _block_index
    return pl.pallas_call(
        functools.partial(_expert_down_body, rpt=rpt),
        grid_spec=pltpu.PrefetchScalarGridSpec(
            num_scalar_prefetch=2,
            grid=(1, n_rows // tm),
            in_specs=[
                pl.BlockSpec((tm, de), lambda j, i, be, nv: (blk(i, nv), 0)),
                pl.BlockSpec((1, de, d), lambda j, i, be, nv: (be[blk(i, nv)], 0, 0)),
                pl.BlockSpec((1, 1, d), lambda j, i, be, nv: (be[blk(i, nv)], 0, 0)),
            ],
            out_specs=pl.BlockSpec((tm * rpt, V7X_LANES), lambda j, i, be, nv: (i, 0)),
            scratch_shapes=[pltpu.VMEM((de, d), _BF16)],
        ),
        out_shape=jax.ShapeDtypeStruct((n_rows * rpt, V7X_LANES), _U32),
        compiler_params=_params("arbitrary", "arbitrary"),
        name="expert_down",
    )(bexp, nvalid, act, w_down, b_down.reshape(n_exp, 1, d))


def _combine_body(pstart_ref, eidx_ref, rank_ref, gate_ref, x2_ref, rows_ref, o_ref, buf, sem,
                  *, tokens, rpt):
    def gather(a, token, choice):
        src = pstart_ref[eidx_ref[a]] + rank_ref[a]
        return _tile_copy(rows_ref, src, buf, choice * tokens + token, sem, rpt)

    _start_and_wait_all(tokens, gather)

    gate = gate_ref[...]
    acc = [None] * (2 * rpt)
    for r in range(TOP_K):
        g = gate[:, r:r + 1]
        for s, slab in enumerate(_load_token_tiles(buf, tokens, rpt, first=r * tokens * rpt)):
            acc[s] = g * slab if acc[s] is None else acc[s] + g * slab
    o_ref[...] = x2_ref[...] + jnp.concatenate(acc, axis=-1)


def _combine(x2, rows, eidx, rank, gate, pstart, *, rpt):
    t, d = x2.shape
    tokens = _pick(t, (256, 128))
    smem = lambda: pl.BlockSpec((tokens * TOP_K,), lambda i, *_: (i,), memory_space=pltpu.SMEM)
    return pl.pallas_call(
        functools.partial(_combine_body, tokens=tokens, rpt=rpt),
        grid_spec=pltpu.PrefetchScalarGridSpec(
            num_scalar_prefetch=1,
            grid=(t // tokens,),
            in_specs=[smem(), smem(),
                      pl.BlockSpec((tokens, V7X_LANES), lambda i, *_: (i, 0)),
                      pl.BlockSpec((tokens, d), lambda i, *_: (i, 0)),
                      pl.BlockSpec(memory_space=pl.ANY)],
            out_specs=pl.BlockSpec((tokens, d), lambda i, *_: (i, 0)),
            scratch_shapes=[pltpu.VMEM((TOP_K * tokens * rpt, V7X_LANES), _U32),
                            pltpu.SemaphoreType.DMA],
        ),
        out_shape=jax.ShapeDtypeStruct((t, d), _F32),
        compiler_params=_params("arbitrary"),
        name="combine",
    )(pstart, eidx, rank, gate, x2, rows)


def _pad_lanes(v, fill=0.0):
    v = v.reshape(1, -1).astype(_F32)
    return jnp.pad(v, ((0, 0), (0, V7X_LANES - v.shape[1])), constant_values=fill)


def kernel(x, lb_logits, norm1, w_in, hg_norm, fox_q_norm, fox_k_norm, fox_f_bias, w_up_hg, w_up_fox,
           w_out, norm2, router_w, router_b, w_gu, b_gu, w_down, b_down):
    batch, seq, d = x.shape
    t = batch * seq
    depth = w_in.shape[0]
    hd = hg_norm.shape[1]
    hg_heads = w_up_hg.shape[1] // hd
    fox_heads = fox_f_bias.shape[1]
    hg_w, fox_w = hg_heads * hd, fox_heads * hd
    n_exp = router_w.shape[2]
    assert hd == V7X_LANES and fox_q_norm.shape[1] == hd and lb_logits.shape[1] == hg_w
    assert n_exp <= V7X_LANES and d % (2 * V7X_LANES) == 0
    assert w_in.shape[2] == 4 * hg_w + 3 * fox_w + fox_heads + 2 * d
    rpt = d // (2 * V7X_LANES)

    ff0 = 4 * hg_w + 3 * fox_w
    slab_ga = 0
    slab_gb = d // V7X_LANES
    slab_hq = 2 * (d // V7X_LANES)
    slab_fq = slab_hq + 4 * hg_heads

    n_rows = t * TOP_K + n_exp * EXPERT_ROWS
    xf = x.reshape(t, d)
    for l in range(depth):
        w = w_in[l]
        w_main = jnp.concatenate([w[:, ff0 + fox_heads:], w[:, :ff0]], axis=1).astype(_BF16)
        w_ff = jnp.pad(w[:, ff0:ff0 + fox_heads], ((0, 0), (0, V7X_LANES - fox_heads)))
        p, ff = _in_proj(xf, norm1[l].reshape(1, d), w_main, w_ff)

        a = _hgrn2(p, lb_logits, hg_norm[l].reshape(1, hd), layer=l, batch=batch, seq=seq,
                   heads=hg_heads, slab_q=slab_hq, slab_f=slab_hq + hg_heads,
                   slab_i=slab_hq + 2 * hg_heads, slab_g=slab_hq + 3 * hg_heads)

        c = _fox_decay(ff, _pad_lanes(fox_f_bias[l]), batch=batch, seq=seq)
        b = _fox_attn(p, c, fox_q_norm[l].reshape(1, hd), fox_k_norm[l].reshape(1, hd),
                      batch=batch, seq=seq, heads=fox_heads, slab_q=slab_fq,
                      slab_k=slab_fq + fox_heads, slab_v=slab_fq + 2 * fox_heads)

        rw_pad = jnp.pad(router_w[l], ((0, 0), (0, V7X_LANES - n_exp)))
        x2, h2, route, gate, counts = _merge(
            a, b, p, xf, w_up_hg[l].astype(_BF16), w_up_fox[l].astype(_BF16),
            w_out[l].astype(_BF16), norm2[l].reshape(1, d), rw_pad,
            _pad_lanes(router_b[l], NEG_BIG), slab_ga=slab_ga, slab_gb=slab_gb, rpt=rpt)

        counts = counts[0, :n_exp]
        padded = (counts + EXPERT_ROWS - 1) // EXPERT_ROWS * EXPERT_ROWS
        upto = jnp.arange(n_exp)[:, None] >= jnp.arange(n_exp)[None, :]
        pend = jnp.sum(jnp.where(upto, padded[None, :], 0), axis=1).astype(_I32)
        pstart = jnp.concatenate([jnp.zeros((1,), _I32), pend])
        n_blocks = n_rows // EXPERT_ROWS
        block_first = jnp.arange(n_blocks, dtype=_I32) * EXPERT_ROWS
        bexp = jnp.minimum(jnp.sum(pend[None, :] <= block_first[:, None], axis=1),
                           n_exp - 1).astype(_I32)
        nvalid = (pend[-1:] // EXPERT_ROWS).astype(_I32)
        eidx = route[:, :TOP_K].reshape(-1)
        rank = route[:, TOP_K:2 * TOP_K].reshape(-1)

        xs = _dispatch(h2, eidx, rank, pstart, counts, n_rows=n_rows, rpt=rpt)
        act = _expert_gu(xs, w_gu[l], b_gu[l], bexp, nvalid, rpt=rpt)
        rows = _expert_down(act, w_down[l], b_down[l], bexp, nvalid, rpt=rpt)
        xf = _combine(x2, rows, eidx, rank, gate, pstart, rpt=rpt)
    return xf.reshape(batch, seq, d)
```
